```python
import math
import jax, jax.numpy as jnp
from jax import lax
import numpy as np

D_MODEL = 4096
BATCH = 2
SEQ = 8192
DEPTH = 2

CHUNK = 128
GM_WIDTH = D_MODEL // 2
GM_GROUPS = 8
GM_GROUP_DIM = GM_WIDTH // GM_GROUPS
DA_HEAD_DIM = 128
DA_V_DIM = 2 * DA_HEAD_DIM
DA_HEADS = D_MODEL // DA_V_DIM
DA_QK_WIDTH = DA_HEADS * 2 * DA_HEAD_DIM
DA_WIDTH = DA_HEADS * DA_V_DIM
D_FF = 4 * D_MODEL
ROPE_THETA = 10000.0
EPS = 1e-6
Q_BLOCK = 128
NEG_INF = -1e30
IN_COLS = 2 * GM_WIDTH + 2 * DA_QK_WIDTH + DA_WIDTH + 2 * D_MODEL
SPLITS = (GM_WIDTH, 2 * GM_WIDTH, 2 * GM_WIDTH + DA_QK_WIDTH,
          2 * GM_WIDTH + 2 * DA_QK_WIDTH, 2 * GM_WIDTH + 2 * DA_QK_WIDTH + DA_WIDTH,
          2 * GM_WIDTH + 2 * DA_QK_WIDTH + DA_WIDTH + D_MODEL)

kernel_name = "hybrid_gmlp_diffattn_gated_block"


def rms_norm(x, g):
    xf = x.astype(jnp.float32)
    y = xf * lax.rsqrt(jnp.mean(xf * xf, axis=-1, keepdims=True) + EPS)
    return (y * g.astype(jnp.float32)).astype(x.dtype)


def layer_norm(x, g, b):
    xf = x.astype(jnp.float32)
    mu = jnp.mean(xf, axis=-1, keepdims=True)
    xc = xf - mu
    y = xc * lax.rsqrt(jnp.mean(xc * xc, axis=-1, keepdims=True) + EPS)
    return (y * g.astype(jnp.float32) + b.astype(jnp.float32)).astype(x.dtype)


def rope_tables(positions):
    inv_freq = ROPE_THETA ** (-jnp.arange(0, DA_HEAD_DIM, 2, dtype=jnp.float32) / DA_HEAD_DIM)
    ang = positions.astype(jnp.float32)[..., None] * inv_freq
    ang = jnp.concatenate([ang, ang], axis=-1)
    return jnp.cos(ang)[:, :, None, None, :], jnp.sin(ang)[:, :, None, None, :]


def apply_rope(x, cos, sin):
    x1, x2 = jnp.split(x, 2, axis=-1)
    rot = jnp.concatenate([-x2, x1], axis=-1)
    return (x.astype(jnp.float32) * cos + rot.astype(jnp.float32) * sin).astype(x.dtype)


def chunked_spatial_gating(u, v, w_s, b_s, ln_g, ln_b):
    B, S, _ = v.shape
    v = layer_norm(v, ln_g, ln_b)
    vc = v.reshape(B, S // CHUNK, CHUNK, GM_GROUPS, GM_GROUP_DIM)
    causal = jnp.tril(jnp.ones((CHUNK, CHUNK), dtype=bool))
    w = jnp.where(causal[None], w_s, 0).astype(v.dtype)
    mixed = jnp.einsum('gts,bcsgd->bctgd', w, vc) + b_s.T[None, None, :, :, None].astype(v.dtype)
    return u * mixed.reshape(B, S, GM_WIDTH)


def diff_attention(q, k, v, lam):
    B, S = q.shape[:2]
    nqb = S // Q_BLOCK
    scale = DA_HEAD_DIM ** -0.5
    qb = jnp.swapaxes(q.reshape(B, nqb, Q_BLOCK, DA_HEADS, 2, DA_HEAD_DIM), 0, 1)
    k_idx = jnp.arange(S)

    def one_block(args):
        q_blk, i = args
        s = jnp.einsum('bqhcd,bkhcd->bhcqk', q_blk, k,
                       preferred_element_type=jnp.float32) * scale
        q_idx = i * Q_BLOCK + jnp.arange(Q_BLOCK)
        mask = k_idx[None, :] <= q_idx[:, None]
        s = jnp.where(mask, s, NEG_INF)
        p = jax.nn.softmax(s, axis=-1)
        p = p[:, :, 0] - lam * p[:, :, 1]
        return jnp.einsum('bhqk,bkhe->bqhe', p.astype(v.dtype), v)

    out = lax.map(one_block, (qb, jnp.arange(nqb)))
    return jnp.swapaxes(out, 0, 1).reshape(B, S, DA_HEADS, DA_V_DIM)


def setup_inputs(seed: int = 0) -> dict:
    key = jax.random.key(seed)
    ks = jax.random.split(key, 24)
    f32 = jnp.float32

    def nrm(k, shape, scale):
        return jax.random.normal(k, shape, dtype=f32) * scale

    def gain(k, shape):
        return 1.0 + 0.02 * jax.random.normal(k, shape, dtype=f32)

    x = jax.random.normal(ks[0], (BATCH, SEQ, D_MODEL), dtype=f32)
    offset = jax.random.randint(ks[1], (BATCH, 1), 0, 4096, dtype=jnp.int32)
    positions = (jnp.arange(SEQ, dtype=jnp.int32)[None, :] + offset).astype(jnp.int32)
    return {
        "x": x,
        "positions": positions,
        "mix_norm_g": gain(ks[2], (DEPTH, D_MODEL)),
        "w_in": nrm(ks[3], (DEPTH, D_MODEL, IN_COLS), D_MODEL ** -0.5),
        "gm_w_s": nrm(ks[4], (DEPTH, GM_GROUPS, CHUNK, CHUNK), CHUNK ** -0.5),
        "gm_b_s": 1.0 + 0.02 * jax.random.normal(ks[5], (DEPTH, GM_GROUPS, CHUNK), dtype=f32),
        "gm_ln_g": gain(ks[6], (DEPTH, GM_WIDTH)),
        "gm_ln_b": nrm(ks[7], (DEPTH, GM_WIDTH), 0.02),
        "q_norm_g": gain(ks[8], (DEPTH, DA_HEAD_DIM)),
        "k_norm_g": gain(ks[9], (DEPTH, DA_HEAD_DIM)),
        "lambda_q1": nrm(ks[10], (DEPTH, DA_HEAD_DIM), 0.1),
        "lambda_k1": nrm(ks[11], (DEPTH, DA_HEAD_DIM), 0.1),
        "lambda_q2": nrm(ks[12], (DEPTH, DA_HEAD_DIM), 0.1),
        "lambda_k2": nrm(ks[13], (DEPTH, DA_HEAD_DIM), 0.1),
        "subln_g": gain(ks[14], (DEPTH, DA_V_DIM)),
        "w_proj_a": nrm(ks[15], (DEPTH, GM_WIDTH, D_MODEL), GM_WIDTH ** -0.5),
        "w_proj_b": nrm(ks[16], (DEPTH, DA_WIDTH, D_MODEL), DA_WIDTH ** -0.5),
        "w_out": nrm(ks[17], (DEPTH, D_MODEL, D_MODEL), 0.5 * D_MODEL ** -0.5),
        "mlp_norm_g": gain(ks[18], (DEPTH, D_MODEL)),
        "w_up": nrm(ks[19], (DEPTH, D_MODEL, D_FF), D_MODEL ** -0.5),
        "w_down": nrm(ks[20], (DEPTH, D_FF, D_MODEL), 0.5 * D_FF ** -0.5),
    }


def reference(x, positions, mix_norm_g, w_in, gm_w_s, gm_b_s, gm_ln_g, gm_ln_b,
              q_norm_g, k_norm_g, lambda_q1, lambda_k1, lambda_q2, lambda_k2, subln_g,
              w_proj_a, w_proj_b, w_out, mlp_norm_g, w_up, w_down):
    B, S, _ = x.shape
    cos, sin = rope_tables(positions)
    for l in range(DEPTH):
        h = rms_norm(x, mix_norm_g[l])
        z = jnp.einsum('bsd,dc->bsc', h, w_in[l])
        u, v, q, k, va, ga, gb = jnp.split(z, SPLITS, axis=-1)

        y_a = chunked_spatial_gating(jax.nn.gelu(u), jax.nn.gelu(v),
                                     gm_w_s[l], gm_b_s[l], gm_ln_g[l], gm_ln_b[l])
        y_a = jnp.einsum('bsc,cd->bsd', y_a, w_proj_a[l])

        q = apply_rope(rms_norm(q.reshape(B, S, DA_HEADS, 2, DA_HEAD_DIM), q_norm_g[l]), cos, sin)
        k = apply_rope(rms_norm(k.reshape(B, S, DA_HEADS, 2, DA_HEAD_DIM), k_norm_g[l]), cos, sin)
        va = va.reshape(B, S, DA_HEADS, DA_V_DIM)
        lambda_init = 0.8 - 0.6 * math.exp(-0.3 * l)
        lam = (jnp.exp(jnp.sum(lambda_q1[l].astype(jnp.float32) * lambda_k1[l].astype(jnp.float32)))
               - jnp.exp(jnp.sum(lambda_q2[l].astype(jnp.float32) * lambda_k2[l].astype(jnp.float32)))
               + lambda_init)
        o = diff_attention(q, k, va, lam)
        o = rms_norm(o, subln_g[l]) * (1.0 - lambda_init)
        y_b = jnp.einsum('bsc,cd->bsd', o.reshape(B, S, DA_WIDTH), w_proj_b[l])

        merged = jax.nn.sigmoid(ga) * y_a + jax.nn.sigmoid(gb) * y_b
        x = x + jnp.einsum('bsd,de->bse', merged, w_out[l])

        h = rms_norm(x, mlp_norm_g[l])
        hid = jnp.square(jax.nn.relu(jnp.einsum('bsd,df->bsf', h, w_up[l])))
        x = x + jnp.einsum('bsf,fd->bsd', hid, w_down[l])
    return x
```

```python
import functools
import math

import jax
import jax.numpy as jnp
from jax import lax
from jax.experimental import pallas as pl
from jax.experimental.pallas import tpu as pltpu

D_MODEL = 4096
CHUNK = 128
GM_WIDTH = D_MODEL // 2
GM_GROUPS = 8
GM_GROUP_DIM = GM_WIDTH // GM_GROUPS
HEAD_DIM = 128
V_DIM = 2 * HEAD_DIM
HEADS = D_MODEL // V_DIM
QK_WIDTH = HEADS * 2 * HEAD_DIM
DA_WIDTH = HEADS * V_DIM
D_FF = 4 * D_MODEL
ROPE_THETA = 10000.0
EPS = 1e-6
NEG_INF = -1e30
COL_U = 0
COL_Q = 2 * GM_WIDTH
COL_K = COL_Q + QK_WIDTH
COL_VA = COL_K + QK_WIDTH
COL_GA = COL_VA + DA_WIDTH
COL_GB = COL_GA + D_MODEL
IN_COLS = COL_GB + D_MODEL

V7X_VMEM_BYTES = 64 * 1024 * 1024
VMEM_CAP_BYTES = V7X_VMEM_BYTES - 8 * 1024 * 1024

F32 = jnp.float32
BF16 = jnp.bfloat16


def _nbytes(shape, dtype):
    return math.prod(shape) * jnp.dtype(dtype).itemsize


def _params(n_grid, block_bytes, scratch_bytes=0):
    need = 2 * block_bytes + scratch_bytes
    limit = min(VMEM_CAP_BYTES, max(need + need // 4, 32 * 1024 * 1024))
    return pltpu.CompilerParams(
        dimension_semantics=("arbitrary",) * n_grid, vmem_limit_bytes=limit)


def _rmsnorm_kernel(x_ref, g_ref, o_ref):
    x = x_ref[...]
    ms = jnp.mean(x * x, axis=-1, keepdims=True)
    o_ref[...] = (x * lax.rsqrt(ms + EPS) * g_ref[...]).astype(o_ref.dtype)


def rmsnorm(x, g, rows=512):
    t, d = x.shape
    blocks = _nbytes((rows, d), F32) + _nbytes((rows, d), BF16)
    return pl.pallas_call(
        _rmsnorm_kernel,
        grid=(t // rows,),
        in_specs=[pl.BlockSpec((rows, d), lambda i: (i, 0)),
                  pl.BlockSpec((1, d), lambda i: (0, 0))],
        out_specs=pl.BlockSpec((rows, d), lambda i: (i, 0)),
        out_shape=jax.ShapeDtypeStruct((t, d), BF16),
        compiler_params=_params(1, blocks, _nbytes((rows, d), F32)),
        name="rmsnorm",
    )(x, g.reshape(1, d))


def _matmul_kernel(*refs, nk, act, has_gate, has_add):
    a_ref, w_ref = refs[0], refs[1]
    pos = 2
    gate_ref = add_ref = None
    if has_gate:
        gate_ref = refs[pos]
        pos += 1
    if has_add:
        add_ref = refs[pos]
        pos += 1
    o_ref = refs[pos]

    part = jnp.dot(a_ref[...], w_ref[...], preferred_element_type=F32)
    if nk == 1:
        r = part
        if act == "relu2":
            r = jnp.square(jnp.maximum(r, 0.0))
        if has_gate:
            r = r * jax.nn.sigmoid(gate_ref[...].astype(F32))
        if has_add:
            r = r + add_ref[...].astype(F32)
        o_ref[...] = r.astype(o_ref.dtype)
        return

    k = pl.program_id(2)

    @pl.when(k == 0)
    def _():
        o_ref[...] = part + add_ref[...] if has_add else part

    @pl.when(k > 0)
    def _():
        o_ref[...] += part


def matmul(a, w, *, out_dtype, act=None, gate=None, gate_col=0, add=None,
           bm=1024, bn=1024, bk=4096, name="matmul"):
    m, kdim = a.shape
    n = w.shape[1]
    bk = min(bk, kdim)
    nk = kdim // bk
    assert m % bm == 0 and n % bn == 0 and kdim % bk == 0 and gate_col % bn == 0
    if nk > 1:
        assert out_dtype == F32 and act is None and gate is None
        assert add is None or add.dtype == F32
        grid = (m // bm, n // bn, nk)
        a_map = lambda i, j, k: (i, k)
        w_map = lambda i, j, k: (k, j)
        o_map = lambda i, j, k: (i, j)
        g_map = lambda i, j, k: (i, gate_col // bn + j)
    else:
        grid = (m // bm, n // bn)
        a_map = lambda i, j: (i, 0)
        w_map = lambda i, j: (0, j)
        o_map = lambda i, j: (i, j)
        g_map = lambda i, j: (i, gate_col // bn + j)
    in_specs = [pl.BlockSpec((bm, bk), a_map), pl.BlockSpec((bk, bn), w_map)]
    args = [a, w]
    blocks = _nbytes((bm, bk), a.dtype) + _nbytes((bk, bn), w.dtype) + _nbytes((bm, bn), out_dtype)
    if gate is not None:
        in_specs.append(pl.BlockSpec((bm, bn), g_map))
        args.append(gate)
        blocks += _nbytes((bm, bn), gate.dtype)
    if add is not None:
        in_specs.append(pl.BlockSpec((bm, bn), o_map))
        args.append(add)
        blocks += _nbytes((bm, bn), add.dtype)
    result_bytes = 2 * _nbytes((bm, bn), F32)
    kern = functools.partial(_matmul_kernel, nk=nk, act=act,
                             has_gate=gate is not None, has_add=add is not None)
    return pl.pallas_call(
        kern,
        grid=grid,
        in_specs=in_specs,
        out_specs=pl.BlockSpec((bm, bn), o_map),
        out_shape=jax.ShapeDtypeStruct((m, n), out_dtype),
        compiler_params=_params(len(grid), blocks, result_bytes),
        name=name,
    )(*args)


def _rope_table_kernel(pos_ref, freq_ref, cos_ref, sin_ref):
    ang = pos_ref[...].astype(F32) * freq_ref[...]
    lane = lax.broadcasted_iota(jnp.int32, ang.shape, 1)
    sign = jnp.where(lane < HEAD_DIM // 2, -1.0, 1.0)
    cos_ref[...] = jnp.cos(ang)
    sin_ref[...] = jnp.sin(ang) * sign


def rope_tables(positions, rows=512):
    t = positions.size
    inv_freq = ROPE_THETA ** (-jnp.arange(0, HEAD_DIM, 2, dtype=F32) / HEAD_DIM)
    freq = jnp.concatenate([inv_freq, inv_freq]).reshape(1, HEAD_DIM)
    tab = jax.ShapeDtypeStruct((t, HEAD_DIM), F32)
    return pl.pallas_call(
        _rope_table_kernel,
        grid=(t // rows,),
        in_specs=[pl.BlockSpec((rows, 1), lambda i: (i, 0)),
                  pl.BlockSpec((1, HEAD_DIM), lambda i: (0, 0))],
        out_specs=[pl.BlockSpec((rows, HEAD_DIM), lambda i: (i, 0))] * 2,
        out_shape=[tab, tab],
        compiler_params=_params(1, 3 * _nbytes((rows, HEAD_DIM), F32)),
        name="rope_tables",
    )(positions.reshape(t, 1), freq)


def _qk_prep_kernel(z_ref, g_ref, cos_ref, sin_ref, o_ref):
    which = pl.program_id(1)
    scale = jnp.where(which == 0, HEAD_DIM ** -0.5, 1.0).astype(F32)
    g = g_ref[0]
    cos = cos_ref[...]
    sin = sin_ref[...]
    for c in range(QK_WIDTH // HEAD_DIM):
        sl = slice(c * HEAD_DIM, (c + 1) * HEAD_DIM)
        x = z_ref[:, sl].astype(F32)
        ms = jnp.mean(x * x, axis=-1, keepdims=True)
        y = x * lax.rsqrt(ms + EPS) * g
        r = y * cos + pltpu.roll(y, HEAD_DIM // 2, 1) * sin
        o_ref[0, :, sl] = (r * scale).astype(o_ref.dtype)


def qk_prep(z, qk_gain, cos, sin, rows=256):
    t = z.shape[0]
    blocks = (2 * _nbytes((rows, QK_WIDTH), BF16) + 2 * _nbytes((rows, HEAD_DIM), F32))
    return pl.pallas_call(
        _qk_prep_kernel,
        grid=(t // rows, 2),
        in_specs=[pl.BlockSpec((rows, QK_WIDTH), lambda i, j: (i, COL_Q // QK_WIDTH + j)),
                  pl.BlockSpec((1, 1, HEAD_DIM), lambda i, j: (j, 0, 0)),
                  pl.BlockSpec((rows, HEAD_DIM), lambda i, j: (i, 0)),
                  pl.BlockSpec((rows, HEAD_DIM), lambda i, j: (i, 0))],
        out_specs=pl.BlockSpec((1, rows, QK_WIDTH), lambda i, j: (j, i, 0)),
        out_shape=jax.ShapeDtypeStruct((2, t, QK_WIDTH), BF16),
        compiler_params=_params(2, blocks),
        name="qk_prep",
    )(z, qk_gain, cos, sin)


def _gmlp_kernel(z_ref, w_ref, b_ref, g_ref, beta_ref, o_ref, vn_ref, *, rows):
    v = jax.nn.gelu(z_ref[:, GM_WIDTH:].astype(F32))
    mu = jnp.mean(v, axis=-1, keepdims=True)
    vc = v - mu
    var = jnp.mean(vc * vc, axis=-1, keepdims=True)
    vn = vc * lax.rsqrt(var + EPS) * g_ref[...] + beta_ref[...]
    vn_ref[...] = vn.astype(BF16)
    t_idx = lax.broadcasted_iota(jnp.int32, (CHUNK, CHUNK), 0)
    s_idx = lax.broadcasted_iota(jnp.int32, (CHUNK, CHUNK), 1)
    causal = s_idx <= t_idx
    for g in range(GM_GROUPS):
        w = jnp.where(causal, w_ref[g], 0.0).astype(BF16)
        bias = b_ref[:, g:g + 1]
        cols = slice(g * GM_GROUP_DIM, (g + 1) * GM_GROUP_DIM)
        for c in range(rows // CHUNK):
            rs = slice(c * CHUNK, (c + 1) * CHUNK)
            mixed = jnp.dot(w, vn_ref[rs, cols], preferred_element_type=F32) + bias
            u = jax.nn.gelu(z_ref[rs, cols].astype(F32))
            o_ref[rs, cols] = (u * mixed).astype(o_ref.dtype)


def gmlp_gating(z, w_s, b_s, ln_g, ln_b, rows=256):
    t = z.shape[0]
    blocks = _nbytes((rows, 2 * GM_WIDTH), BF16) + _nbytes((rows, GM_WIDTH), BF16)
    small = _nbytes(w_s.shape, F32) + _nbytes((CHUNK, 128), F32) + 2 * _nbytes((8, GM_WIDTH), F32)
    return pl.pallas_call(
        functools.partial(_gmlp_kernel, rows=rows),
        grid=(t // rows,),
        in_specs=[pl.BlockSpec((rows, 2 * GM_WIDTH), lambda i: (i, COL_U)),
                  pl.BlockSpec(w_s.shape, lambda i: (0, 0, 0)),
                  pl.BlockSpec((CHUNK, GM_GROUPS), lambda i: (0, 0)),
                  pl.BlockSpec((1, GM_WIDTH), lambda i: (0, 0)),
                  pl.BlockSpec((1, GM_WIDTH), lambda i: (0, 0))],
        out_specs=pl.BlockSpec((rows, GM_WIDTH), lambda i: (i, 0)),
        out_shape=jax.ShapeDtypeStruct((t, GM_WIDTH), BF16),
        scratch_shapes=[pltpu.VMEM((rows, GM_WIDTH), BF16)],
        compiler_params=_params(1, blocks + small, 4 * _nbytes((rows, GM_WIDTH), F32)),
        name="gmlp_gating",
    )(z, w_s, b_s.T, ln_g.reshape(1, GM_WIDTH), ln_b.reshape(1, GM_WIDTH))


def _attn_kernel(lam_ref, q_ref, k_ref, v_ref, g_ref, o_ref, m_ref, l_ref, acc_ref,
                 *, blk, lambda_init):
    i = pl.program_id(2)
    m_ref[...] = jnp.full(m_ref.shape, NEG_INF, F32)
    l_ref[...] = jnp.zeros(l_ref.shape, F32)
    acc_ref[...] = jnp.zeros(acc_ref.shape, F32)

    def tile(j, masked):
        rows = pl.ds(pl.multiple_of(j * blk, blk), blk)
        v = v_ref[rows, :]
        for c in range(2):
            cs = slice(c * HEAD_DIM, (c + 1) * HEAD_DIM)
            s = lax.dot_general(q_ref[0, :, cs], k_ref[0, rows, cs],
                                (((1,), (1,)), ((), ())), preferred_element_type=F32)
            if masked:
                qi = lax.broadcasted_iota(jnp.int32, s.shape, 0)
                ki = lax.broadcasted_iota(jnp.int32, s.shape, 1)
                s = jnp.where(ki <= qi, s, NEG_INF)
            m_prev = m_ref[c]
            m_new = jnp.maximum(m_prev, jnp.max(s, axis=-1, keepdims=True))
            alpha = jnp.exp(m_prev - m_new)
            p = jnp.exp(s - m_new)
            l_ref[c] = alpha * l_ref[c] + jnp.sum(p, axis=-1, keepdims=True)
            acc_ref[c] = alpha * acc_ref[c] + jnp.dot(p.astype(BF16), v,
                                                       preferred_element_type=F32)
            m_ref[c] = m_new

    def body(j, carry):
        tile(j, masked=False)
        return carry

    lax.fori_loop(0, i, body, 0)
    tile(i, masked=True)

    lp = lam_ref[...]
    lam = (jnp.exp(jnp.sum(lp[0:1] * lp[1:2], axis=-1, keepdims=True))
           - jnp.exp(jnp.sum(lp[2:3] * lp[3:4], axis=-1, keepdims=True))
           + lambda_init)
    o = acc_ref[0] / l_ref[0] - lam * (acc_ref[1] / l_ref[1])
    ms = jnp.mean(o * o, axis=-1, keepdims=True)
    o = o * lax.rsqrt(ms + EPS) * g_ref[...]
    o_ref[...] = (o * (1.0 - lambda_init)).astype(o_ref.dtype)


def diff_attention(qk, z, lam_params, subln_g, *, batch, seq, lambda_init, blk=512):
    t = batch * seq
    nq = seq // blk
    blocks = (2 * _nbytes((blk, V_DIM), BF16) + 2 * _nbytes((seq, V_DIM), BF16))
    scratch = 2 * _nbytes((blk, V_DIM), F32) + 4 * _nbytes((blk, 128), F32)
    tiles = 6 * _nbytes((blk, blk), F32)
    kern = functools.partial(_attn_kernel, blk=blk, lambda_init=lambda_init)
    return pl.pallas_call(
        kern,
        grid=(batch, HEADS, nq),
        in_specs=[pl.BlockSpec((4, HEAD_DIM), lambda b, h, i: (0, 0)),
                  pl.BlockSpec((1, blk, V_DIM), lambda b, h, i: (0, b * nq + i, h)),
                  pl.BlockSpec((1, seq, V_DIM), lambda b, h, i: (1, b, h)),
                  pl.BlockSpec((seq, V_DIM), lambda b, h, i: (b, COL_VA // V_DIM + h)),
                  pl.BlockSpec((1, V_DIM), lambda b, h, i: (0, 0))],
        out_specs=pl.BlockSpec((blk, V_DIM), lambda b, h, i: (b * nq + i, h)),
        out_shape=jax.ShapeDtypeStruct((t, DA_WIDTH), BF16),
        scratch_shapes=[pltpu.VMEM((2, blk, 1), F32),
                        pltpu.VMEM((2, blk, 1), F32),
                        pltpu.VMEM((2, blk, V_DIM), F32)],
        compiler_params=_params(3, blocks, scratch + tiles),
        name="diff_attention",
    )(lam_params, qk, qk, z, subln_g.reshape(1, V_DIM))


def kernel(x, positions, mix_norm_g, w_in, gm_w_s, gm_b_s, gm_ln_g, gm_ln_b, q_norm_g, k_norm_g,
           lambda_q1, lambda_k1, lambda_q2, lambda_k2, subln_g, w_proj_a, w_proj_b, w_out,
           mlp_norm_g, w_up, w_down):
    batch, seq, d = x.shape
    depth = w_in.shape[0]
    t = batch * seq
    xf = x.reshape(t, d)
    cos, sin = rope_tables(positions)
    for l in range(depth):
        lambda_init = 0.8 - 0.6 * math.exp(-0.3 * l)
        h = rmsnorm(xf, mix_norm_g[l])
        z = matmul(h, w_in[l].astype(BF16), out_dtype=BF16, name="in_proj")
        ya = gmlp_gating(z, gm_w_s[l], gm_b_s[l], gm_ln_g[l], gm_ln_b[l])
        qk_gain = jnp.stack([q_norm_g[l], k_norm_g[l]]).reshape(2, 1, HEAD_DIM)
        qk = qk_prep(z, qk_gain, cos, sin)
        lam_params = jnp.stack([lambda_q1[l], lambda_k1[l], lambda_q2[l], lambda_k2[l]])
        o = diff_attention(qk, z, lam_params, subln_g[l], batch=batch, seq=seq,
                           lambda_init=lambda_init)
        ga = matmul(ya, w_proj_a[l].astype(BF16), out_dtype=F32, gate=z, gate_col=COL_GA,
                    name="proj_a_gate")
        merged = matmul(o, w_proj_b[l].astype(BF16), out_dtype=BF16, gate=z, gate_col=COL_GB,
                        add=ga, bn=512, name="proj_b_gate_merge")
        xf = matmul(merged, w_out[l].astype(BF16), out_dtype=F32, add=xf, bn=512,
                    name="out_proj")
        h = rmsnorm(xf, mlp_norm_g[l])
        hid = matmul(h, w_up[l].astype(BF16), out_dtype=BF16, act="relu2", name="mlp_up")
        xf = matmul(hid, w_down[l].astype(BF16), out_dtype=F32, add=xf, bk=2048,
                    name="mlp_down")
    return xf.reshape(batch, seq, d)
```

```python
import functools
import math

import jax
import jax.numpy as jnp
from jax import lax
from jax.experimental import pallas as pl
from jax.experimental.pallas import tpu as pltpu

D_MODEL = 4096
CHUNK = 128
GM_WIDTH = D_MODEL // 2
GM_GROUPS = 8
GM_GROUP_DIM = GM_WIDTH // GM_GROUPS
HEAD_DIM = 128
V_DIM = 2 * HEAD_DIM
HEADS = D_MODEL // V_DIM
QK_WIDTH = HEADS * 2 * HEAD_DIM
DA_WIDTH = HEADS * V_DIM
D_FF = 4 * D_MODEL
ROPE_THETA = 10000.0
EPS = 1e-6
NEG_INF = -1e30
COL_U = 0
COL_Q = 2 * GM_WIDTH
COL_K = COL_Q + QK_WIDTH
COL_VA = COL_K + QK_WIDTH
COL_GA = COL_VA + DA_WIDTH
COL_GB = COL_GA + D_MODEL
IN_COLS = COL_GB + D_MODEL

V7X_VMEM_BYTES = 64 * 1024 * 1024
VMEM_CAP_BYTES = V7X_VMEM_BYTES - 8 * 1024 * 1024

F32 = jnp.float32
BF16 = jnp.bfloat16


def _nbytes(shape, dtype):
    return math.prod(shape) * jnp.dtype(dtype).itemsize


def _params(n_grid, block_bytes, scratch_bytes=0):
    need = 2 * block_bytes + scratch_bytes
    limit = min(VMEM_CAP_BYTES, max(need + need // 4, 32 * 1024 * 1024))
    return pltpu.CompilerParams(
        dimension_semantics=("arbitrary",) * n_grid, vmem_limit_bytes=limit)


def _rmsnorm_kernel(x_ref, g_ref, o_ref):
    x = x_ref[...]
    ms = jnp.mean(x * x, axis=-1, keepdims=True)
    o_ref[...] = (x * lax.rsqrt(ms + EPS) * g_ref[...]).astype(o_ref.dtype)


def rmsnorm(x, g, rows=512):
    t, d = x.shape
    blocks = _nbytes((rows, d), F32) + _nbytes((rows, d), BF16)
    return pl.pallas_call(
        _rmsnorm_kernel,
        grid=(t // rows,),
        in_specs=[pl.BlockSpec((rows, d), lambda i: (i, 0)),
                  pl.BlockSpec((1, d), lambda i: (0, 0))],
        out_specs=pl.BlockSpec((rows, d), lambda i: (i, 0)),
        out_shape=jax.ShapeDtypeStruct((t, d), BF16),
        compiler_params=_params(1, blocks, _nbytes((rows, d), F32)),
        name="rmsnorm",
    )(x, g.reshape(1, d))


def _matmul_kernel(*refs, act, has_gate, has_add):
    a_ref, w_ref = refs[0], refs[1]
    pos = 2
    gate_ref = add_ref = None
    if has_gate:
        gate_ref = refs[pos]
        pos += 1
    if has_add:
        add_ref = refs[pos]
        pos += 1
    o_ref, wbf_ref = refs[pos], refs[pos + 1]

    @pl.when(pl.program_id(1) == 0)
    def _():
        wbf_ref[...] = w_ref[0].astype(BF16)

    r = jnp.dot(a_ref[...], wbf_ref[...], preferred_element_type=F32)
    if act == "relu2":
        r = jnp.square(jnp.maximum(r, 0.0))
    if has_gate:
        r = r * jax.nn.sigmoid(gate_ref[...].astype(F32))
    if has_add:
        r = r + add_ref[...].astype(F32)
    o_ref[...] = r.astype(o_ref.dtype)


def matmul(a, w, layer, *, out_dtype, act=None, gate=None, gate_col=0, add=None,
           bm=1024, bn=512, name="matmul"):
    m, kdim = a.shape
    n = w.shape[2]
    assert m % bm == 0 and n % bn == 0 and gate_col % bn == 0
    o_map = lambda j, i: (i, j)
    in_specs = [pl.BlockSpec((bm, kdim), lambda j, i: (i, 0)),
                pl.BlockSpec((1, kdim, bn), lambda j, i: (layer, 0, j))]
    args = [a, w]
    blocks = (_nbytes((bm, kdim), a.dtype) + _nbytes((kdim, bn), w.dtype)
              + _nbytes((bm, bn), out_dtype))
    if gate is not None:
        in_specs.append(pl.BlockSpec((bm, bn), lambda j, i: (i, gate_col // bn + j)))
        args.append(gate)
        blocks += _nbytes((bm, bn), gate.dtype)
    if add is not None:
        in_specs.append(pl.BlockSpec((bm, bn), o_map))
        args.append(add)
        blocks += _nbytes((bm, bn), add.dtype)
    scratch_bytes = _nbytes((kdim, bn), BF16) + 2 * _nbytes((bm, bn), F32)
    kern = functools.partial(_matmul_kernel, act=act,
                             has_gate=gate is not None, has_add=add is not None)
    return pl.pallas_call(
        kern,
        grid=(n // bn, m // bm),
        in_specs=in_specs,
        out_specs=pl.BlockSpec((bm, bn), o_map),
        out_shape=jax.ShapeDtypeStruct((m, n), out_dtype),
        scratch_shapes=[pltpu.VMEM((kdim, bn), BF16)],
        compiler_params=_params(2, blocks, scratch_bytes),
        name=name,
    )(*args)


def _matmul_ktiled_kernel(a_ref, w_ref, add_ref, o_ref):
    @pl.when(pl.program_id(2) == 0)
    def _():
        o_ref[...] = add_ref[...]

    o_ref[...] += jnp.dot(a_ref[...], w_ref[...], preferred_element_type=F32)


def matmul_ktiled(a, w, add, *, bm=1024, bn=512, bk=4096, name="matmul_ktiled"):
    m, kdim = a.shape
    n = w.shape[1]
    assert m % bm == 0 and n % bn == 0 and kdim % bk == 0 and add.dtype == F32
    o_map = lambda i, j, k: (i, j)
    blocks = (_nbytes((bm, bk), a.dtype) + _nbytes((bk, bn), w.dtype)
              + 2 * _nbytes((bm, bn), F32))
    return pl.pallas_call(
        _matmul_ktiled_kernel,
        grid=(m // bm, n // bn, kdim // bk),
        in_specs=[pl.BlockSpec((bm, bk), lambda i, j, k: (i, k)),
                  pl.BlockSpec((bk, bn), lambda i, j, k: (k, j)),
                  pl.BlockSpec((bm, bn), o_map)],
        out_specs=pl.BlockSpec((bm, bn), o_map),
        out_shape=jax.ShapeDtypeStruct((m, n), F32),
        compiler_params=_params(3, blocks, 2 * _nbytes((bm, bn), F32)),
        name=name,
    )(a, w, add)


def _rope_table_kernel(pos_ref, freq_ref, cos_ref, sin_ref):
    ang = pos_ref[...].astype(F32) * freq_ref[...]
    lane = lax.broadcasted_iota(jnp.int32, ang.shape, 1)
    sign = jnp.where(lane < HEAD_DIM // 2, -1.0, 1.0)
    cos_ref[...] = jnp.cos(ang)
    sin_ref[...] = jnp.sin(ang) * sign


def rope_tables(positions, rows=512):
    t = positions.size
    inv_freq = ROPE_THETA ** (-jnp.arange(0, HEAD_DIM, 2, dtype=F32) / HEAD_DIM)
    freq = jnp.concatenate([inv_freq, inv_freq]).reshape(1, HEAD_DIM)
    tab = jax.ShapeDtypeStruct((t, HEAD_DIM), F32)
    return pl.pallas_call(
        _rope_table_kernel,
        grid=(t // rows,),
        in_specs=[pl.BlockSpec((rows, 1), lambda i: (i, 0)),
                  pl.BlockSpec((1, HEAD_DIM), lambda i: (0, 0))],
        out_specs=[pl.BlockSpec((rows, HEAD_DIM), lambda i: (i, 0))] * 2,
        out_shape=[tab, tab],
        compiler_params=_params(1, 3 * _nbytes((rows, HEAD_DIM), F32)),
        name="rope_tables",
    )(positions.reshape(t, 1), freq)


Q_PRESCALE = HEAD_DIM ** -0.5 * math.log2(math.e)


def _qk_prep_kernel(zq_ref, zk_ref, g_ref, cos_ref, sin_ref, q_ref, kt_ref):
    cos = cos_ref[...]
    sin = sin_ref[...]

    def norm_rope(x, g):
        ms = jnp.mean(x * x, axis=-1, keepdims=True)
        y = x * lax.rsqrt(ms + EPS) * g
        return y * cos + pltpu.roll(y, HEAD_DIM // 2, 1) * sin

    for f in range(QK_WIDTH // HEAD_DIM):
        sl = slice(f * HEAD_DIM, (f + 1) * HEAD_DIM)
        q = norm_rope(zq_ref[:, sl].astype(F32), g_ref[0])
        q_ref[:, sl] = (q * Q_PRESCALE).astype(q_ref.dtype)
        k = norm_rope(zk_ref[:, sl].astype(F32), g_ref[1])
        kt_ref[0, f, 0] = k.T.astype(kt_ref.dtype)


def qk_prep(z, qk_gain, cos, sin, *, batch, seq, blk):
    t = z.shape[0]
    nk = seq // blk
    nf = QK_WIDTH // HEAD_DIM
    blocks = (4 * _nbytes((blk, QK_WIDTH), BF16) + 2 * _nbytes((blk, HEAD_DIM), F32))
    return pl.pallas_call(
        _qk_prep_kernel,
        grid=(t // blk,),
        in_specs=[pl.BlockSpec((blk, QK_WIDTH), lambda i: (i, COL_Q // QK_WIDTH)),
                  pl.BlockSpec((blk, QK_WIDTH), lambda i: (i, COL_K // QK_WIDTH)),
                  pl.BlockSpec((2, 1, HEAD_DIM), lambda i: (0, 0, 0)),
                  pl.BlockSpec((blk, HEAD_DIM), lambda i: (i, 0)),
                  pl.BlockSpec((blk, HEAD_DIM), lambda i: (i, 0))],
        out_specs=[pl.BlockSpec((blk, QK_WIDTH), lambda i: (i, 0)),
                   pl.BlockSpec((1, nf, 1, HEAD_DIM, blk), lambda i: (i // nk, 0, i % nk, 0, 0))],
        out_shape=[jax.ShapeDtypeStruct((t, QK_WIDTH), BF16),
                   jax.ShapeDtypeStruct((batch, nf, nk, HEAD_DIM, blk), BF16)],
        compiler_params=_params(1, blocks, 8 * _nbytes((blk, HEAD_DIM), F32)),
        name="qk_prep",
    )(z, z, qk_gain, cos, sin)


def _gmlp_kernel(z_ref, w_ref, b_ref, g_ref, beta_ref, o_ref, vn_ref, *, rows):
    v = jax.nn.gelu(z_ref[:, GM_WIDTH:].astype(F32))
    mu = jnp.mean(v, axis=-1, keepdims=True)
    vc = v - mu
    var = jnp.mean(vc * vc, axis=-1, keepdims=True)
    vn = vc * lax.rsqrt(var + EPS) * g_ref[...] + beta_ref[...]
    vn_ref[...] = vn.astype(BF16)
    t_idx = lax.broadcasted_iota(jnp.int32, (CHUNK, CHUNK), 0)
    s_idx = lax.broadcasted_iota(jnp.int32, (CHUNK, CHUNK), 1)
    causal = s_idx <= t_idx
    for g in range(GM_GROUPS):
        w = jnp.where(causal, w_ref[g], 0.0).astype(BF16)
        bias = b_ref[:, g:g + 1]
        cols = slice(g * GM_GROUP_DIM, (g + 1) * GM_GROUP_DIM)
        for c in range(rows // CHUNK):
            rs = slice(c * CHUNK, (c + 1) * CHUNK)
            mixed = jnp.dot(w, vn_ref[rs, cols], preferred_element_type=F32) + bias
            u = jax.nn.gelu(z_ref[rs, cols].astype(F32))
            o_ref[rs, cols] = (u * mixed).astype(o_ref.dtype)


def gmlp_gating(z, w_s, b_s, ln_g, ln_b, rows=256):
    t = z.shape[0]
    blocks = _nbytes((rows, 2 * GM_WIDTH), BF16) + _nbytes((rows, GM_WIDTH), BF16)
    small = _nbytes(w_s.shape, F32) + _nbytes((CHUNK, 128), F32) + 2 * _nbytes((8, GM_WIDTH), F32)
    return pl.pallas_call(
        functools.partial(_gmlp_kernel, rows=rows),
        grid=(t // rows,),
        in_specs=[pl.BlockSpec((rows, 2 * GM_WIDTH), lambda i: (i, COL_U)),
                  pl.BlockSpec(w_s.shape, lambda i: (0, 0, 0)),
                  pl.BlockSpec((CHUNK, GM_GROUPS), lambda i: (0, 0)),
                  pl.BlockSpec((1, GM_WIDTH), lambda i: (0, 0)),
                  pl.BlockSpec((1, GM_WIDTH), lambda i: (0, 0))],
        out_specs=pl.BlockSpec((rows, GM_WIDTH), lambda i: (i, 0)),
        out_shape=jax.ShapeDtypeStruct((t, GM_WIDTH), BF16),
        scratch_shapes=[pltpu.VMEM((rows, GM_WIDTH), BF16)],
        compiler_params=_params(1, blocks + small, 4 * _nbytes((rows, GM_WIDTH), F32)),
        name="gmlp_gating",
    )(z, w_s, b_s.T, ln_g.reshape(1, GM_WIDTH), ln_b.reshape(1, GM_WIDTH))


LANES = 128
ATTN_BQ = 1024
ATTN_BK = 512
ATTN_RQ = 128
ATTN_UNROLL = 2


def _attn_kernel(lam_ref, q_ref, kt_ref, v_ref, g_ref, o_ref, m_ref, l_ref, acc_ref,
                 *, lambda_init):
    bq, bk, rq = ATTN_BQ, ATTN_BK, ATTN_RQ
    i = pl.program_id(2)
    m_ref[...] = jnp.full(m_ref.shape, NEG_INF, F32)
    l_ref[...] = jnp.zeros(l_ref.shape, F32)
    acc_ref[...] = jnp.zeros(acc_ref.shape, F32)

    def chain(j, c, r, masked, col0):
        kv_rows = pl.ds(pl.multiple_of(j * bk, bk), bk)
        rs = slice(r * rq, (r + 1) * rq)
        q = q_ref[rs, c * HEAD_DIM:(c + 1) * HEAD_DIM]
        s = jnp.dot(q, kt_ref[0, c, j], preferred_element_type=F32)
        if masked:
            qi = lax.broadcasted_iota(jnp.int32, s.shape, 0) + r * rq
            ki = lax.broadcasted_iota(jnp.int32, s.shape, 1) + col0
            s = jnp.where(ki <= qi, s, NEG_INF)
        m_prev = m_ref[c, rs, :]
        m_new = jnp.maximum(m_prev, jnp.max(s, axis=-1, keepdims=True))
        alpha = jnp.exp2(m_prev - m_new)
        parts = [jnp.exp2(s[:, t * LANES:(t + 1) * LANES] - m_new) for t in range(bk // LANES)]
        l_ref[c, rs, :] = alpha * l_ref[c, rs, :] + functools.reduce(jnp.add, parts)
        p = jnp.concatenate(parts, axis=-1).astype(BF16)
        pv = jnp.dot(p, v_ref[kv_rows, :], preferred_element_type=F32)
        alpha_v = jnp.concatenate([alpha] * (V_DIM // LANES), axis=-1)
        acc_ref[c, rs, :] = alpha_v * acc_ref[c, rs, :] + pv
        m_ref[c, rs, :] = m_new

    n_diag = bq // bk
    n_full = i * n_diag

    def body(jj, carry):
        for u in range(ATTN_UNROLL):
            for r in range(bq // rq):
                for c in range(2):
                    chain(jj * ATTN_UNROLL + u, c, r, False, 0)
        return carry

    lax.fori_loop(0, n_full // ATTN_UNROLL, body, 0)

    for d in range(n_diag):
        for r in range(bq // rq):
            row_lo, row_hi = r * rq, (r + 1) * rq - 1
            col_lo, col_hi = d * bk, (d + 1) * bk - 1
            if row_hi < col_lo:
                continue
            for c in range(2):
                chain(n_full + d, c, r, col_hi > row_lo, col_lo)

    lp = lam_ref[...]
    lam = (jnp.exp(jnp.sum(lp[0:1] * lp[1:2], axis=-1, keepdims=True))
           - jnp.exp(jnp.sum(lp[2:3] * lp[3:4], axis=-1, keepdims=True))
           + lambda_init)
    l0 = jnp.sum(l_ref[0], axis=-1, keepdims=True)
    l1 = jnp.sum(l_ref[1], axis=-1, keepdims=True)
    o = acc_ref[0] / l0 - lam * (acc_ref[1] / l1)
    ms = jnp.mean(o * o, axis=-1, keepdims=True)
    o = o * lax.rsqrt(ms + EPS) * g_ref[...]
    o_ref[...] = (o * (1.0 - lambda_init)).astype(o_ref.dtype)


def diff_attention(q, kt, z, lam_params, subln_g, *, batch, seq, lambda_init):
    bq, bk = ATTN_BQ, ATTN_BK
    assert bk % ATTN_RQ == 0 and bq % bk == 0 and (bq // bk) % ATTN_UNROLL == 0 and seq % bq == 0
    t = batch * seq
    nq = seq // bq
    nk = seq // bk
    blocks = (2 * _nbytes((bq, V_DIM), BF16) + 2 * _nbytes((seq, V_DIM), BF16))
    scratch = 2 * _nbytes((bq, V_DIM), F32) + 4 * _nbytes((bq, LANES), F32)
    kern = functools.partial(_attn_kernel, lambda_init=lambda_init)
    return pl.pallas_call(
        kern,
        grid=(batch, HEADS, nq),
        in_specs=[pl.BlockSpec((4, HEAD_DIM), lambda b, h, i: (0, 0)),
                  pl.BlockSpec((bq, V_DIM), lambda b, h, i: (b * nq + i, h)),
                  pl.BlockSpec((1, 2, nk, HEAD_DIM, bk), lambda b, h, i: (b, h, 0, 0, 0)),
                  pl.BlockSpec((seq, V_DIM), lambda b, h, i: (b, COL_VA // V_DIM + h)),
                  pl.BlockSpec((1, V_DIM), lambda b, h, i: (0, 0))],
        out_specs=pl.BlockSpec((bq, V_DIM), lambda b, h, i: (b * nq + i, h)),
        out_shape=jax.ShapeDtypeStruct((t, DA_WIDTH), BF16),
        scratch_shapes=[pltpu.VMEM((2, bq, LANES), F32),
                        pltpu.VMEM((2, bq, LANES), F32),
                        pltpu.VMEM((2, bq, V_DIM), F32)],
        compiler_params=_params(3, blocks, 2 * scratch),
        name="diff_attention",
    )(lam_params, q, kt, z, subln_g.reshape(1, V_DIM))


def kernel(x, positions, mix_norm_g, w_in, gm_w_s, gm_b_s, gm_ln_g, gm_ln_b, q_norm_g, k_norm_g,
           lambda_q1, lambda_k1, lambda_q2, lambda_k2, subln_g, w_proj_a, w_proj_b, w_out,
           mlp_norm_g, w_up, w_down):
    batch, seq, d = x.shape
    depth = w_in.shape[0]
    t = batch * seq
    xf = x.reshape(t, d)
    cos, sin = rope_tables(positions)
    for l in range(depth):
        lambda_init = 0.8 - 0.6 * math.exp(-0.3 * l)
        h = rmsnorm(xf, mix_norm_g[l])
        z = matmul(h, w_in, l, out_dtype=BF16, name="in_proj")
        ya = gmlp_gating(z, gm_w_s[l], gm_b_s[l], gm_ln_g[l], gm_ln_b[l])
        qk_gain = jnp.stack([q_norm_g[l], k_norm_g[l]]).reshape(2, 1, HEAD_DIM)
        q, kt = qk_prep(z, qk_gain, cos, sin, batch=batch, seq=seq, blk=ATTN_BK)
        lam_params = jnp.stack([lambda_q1[l], lambda_k1[l], lambda_q2[l], lambda_k2[l]])
        o = diff_attention(q, kt, z, lam_params, subln_g[l], batch=batch, seq=seq,
                           lambda_init=lambda_init)
        ga = matmul(ya, w_proj_a, l, out_dtype=F32, gate=z, gate_col=COL_GA,
                    name="proj_a_gate")
        merged = matmul(o, w_proj_b, l, out_dtype=BF16, gate=z, gate_col=COL_GB,
                        add=ga, bm=512, name="proj_b_gate_merge")
        xf = matmul(merged, w_out, l, out_dtype=F32, add=xf, bm=512, name="out_proj")
        h = rmsnorm(xf, mlp_norm_g[l])
        hid = matmul(h, w_up, l, out_dtype=BF16, act="relu2", name="mlp_up")
        xf = matmul_ktiled(hid, w_down[l].astype(BF16), xf, name="mlp_down")
    return xf.reshape(batch, seq, d)
```

```python
import functools
import math

import jax
import jax.numpy as jnp
from jax import lax
from jax.experimental import pallas as pl
from jax.experimental.pallas import tpu as pltpu

D_MODEL = 4096
CHUNK = 128
GM_WIDTH = D_MODEL // 2
GM_GROUPS = 8
GM_GROUP_DIM = GM_WIDTH // GM_GROUPS
HEAD_DIM = 128
V_DIM = 2 * HEAD_DIM
HEADS = D_MODEL // V_DIM
QK_WIDTH = HEADS * 2 * HEAD_DIM
DA_WIDTH = HEADS * V_DIM
D_FF = 4 * D_MODEL
ROPE_THETA = 10000.0
EPS = 1e-6
NEG_INF = -1e30
COL_U = 0
COL_Q = 2 * GM_WIDTH
COL_K = COL_Q + QK_WIDTH
COL_VA = COL_K + QK_WIDTH
COL_GA = COL_VA + DA_WIDTH
COL_GB = COL_GA + D_MODEL
IN_COLS = COL_GB + D_MODEL

V7X_VMEM_BYTES = 64 * 1024 * 1024
VMEM_CAP_BYTES = V7X_VMEM_BYTES - 8 * 1024 * 1024

F32 = jnp.float32
BF16 = jnp.bfloat16


def _nbytes(shape, dtype):
    return math.prod(shape) * jnp.dtype(dtype).itemsize


def _params(n_grid, block_bytes, scratch_bytes=0):
    need = 2 * block_bytes + scratch_bytes
    limit = min(VMEM_CAP_BYTES, max(need + need // 4, 32 * 1024 * 1024))
    return pltpu.CompilerParams(
        dimension_semantics=("arbitrary",) * n_grid, vmem_limit_bytes=limit)


def _rmsnorm_kernel(x_ref, g_ref, o_ref):
    x = x_ref[...]
    ms = jnp.mean(x * x, axis=-1, keepdims=True)
    o_ref[...] = (x * lax.rsqrt(ms + EPS) * g_ref[...]).astype(o_ref.dtype)


def rmsnorm(x, g, rows=512):
    t, d = x.shape
    blocks = _nbytes((rows, d), F32) + _nbytes((rows, d), BF16)
    return pl.pallas_call(
        _rmsnorm_kernel,
        grid=(t // rows,),
        in_specs=[pl.BlockSpec((rows, d), lambda i: (i, 0)),
                  pl.BlockSpec((1, d), lambda i: (0, 0))],
        out_specs=pl.BlockSpec((rows, d), lambda i: (i, 0)),
        out_shape=jax.ShapeDtypeStruct((t, d), BF16),
        compiler_params=_params(1, blocks, _nbytes((rows, d), F32)),
        name="rmsnorm",
    )(x, g.reshape(1, d))


def _cast_block_rows(rows, n_steps):
    rb = 16
    while rows // rb > n_steps:
        rb *= 2
    assert rows % rb == 0
    return rb


def _side_cast_specs(casts, n_steps, step_index):
    in_specs, out_specs, out_shapes, nbytes = [], [], [], 0
    for src, layer in casts:
        _, rows, cols = src.shape
        rb = _cast_block_rows(rows, n_steps)
        last = rows // rb - 1

        def in_map(*g, layer=layer, last=last):
            return (layer, jnp.minimum(step_index(*g), last), 0)

        def out_map(*g, last=last):
            return (jnp.minimum(step_index(*g), last), 0)

        in_specs.append(pl.BlockSpec((1, rb, cols), in_map))
        out_specs.append(pl.BlockSpec((rb, cols), out_map))
        out_shapes.append(jax.ShapeDtypeStruct((rows, cols), BF16))
        nbytes += _nbytes((rb, cols), F32) + _nbytes((rb, cols), BF16)
    return in_specs, out_specs, out_shapes, nbytes


def _run_side_casts(src_refs, dst_refs):
    for src, dst in zip(src_refs, dst_refs):
        dst[...] = src[0].astype(BF16)


def _matmul_kernel(*refs, act, has_gate, has_add, n_cast):
    a_ref, w_ref = refs[0], refs[1]
    pos = 2
    gate_ref = add_ref = None
    if has_gate:
        gate_ref = refs[pos]
        pos += 1
    if has_add:
        add_ref = refs[pos]
        pos += 1
    src_refs = refs[pos:pos + n_cast]
    o_ref = refs[pos + n_cast]
    dst_refs = refs[pos + n_cast + 1:]
    _run_side_casts(src_refs, dst_refs)
    r = jnp.dot(a_ref[...], w_ref[...], preferred_element_type=F32)
    if act == "relu2":
        r = jnp.square(jnp.maximum(r, 0.0))
    if has_gate:
        r = r * jax.nn.sigmoid(gate_ref[...].astype(F32))
    if has_add:
        r = r + add_ref[...].astype(F32)
    o_ref[...] = r.astype(o_ref.dtype)


def matmul(a, w, *, out_dtype, act=None, gate=None, gate_col=0, add=None, casts=(),
           bm=1024, bn=1024, name="matmul"):
    m, kdim = a.shape
    n = w.shape[1]
    assert m % bm == 0 and n % bn == 0 and gate_col % bn == 0
    nj = n // bn
    o_map = lambda i, j: (i, j)
    in_specs = [pl.BlockSpec((bm, kdim), lambda i, j: (i, 0)),
                pl.BlockSpec((kdim, bn), lambda i, j: (0, j))]
    args = [a, w]
    blocks = (_nbytes((bm, kdim), a.dtype) + _nbytes((kdim, bn), w.dtype)
              + _nbytes((bm, bn), out_dtype))
    if gate is not None:
        in_specs.append(pl.BlockSpec((bm, bn), lambda i, j: (i, gate_col // bn + j)))
        args.append(gate)
        blocks += _nbytes((bm, bn), gate.dtype)
    if add is not None:
        in_specs.append(pl.BlockSpec((bm, bn), o_map))
        args.append(add)
        blocks += _nbytes((bm, bn), add.dtype)
    c_in, c_out, c_shapes, c_bytes = _side_cast_specs(
        casts, (m // bm) * nj, lambda i, j: i * nj + j)
    kern = functools.partial(_matmul_kernel, act=act, has_gate=gate is not None,
                             has_add=add is not None, n_cast=len(casts))
    out = pl.pallas_call(
        kern,
        grid=(m // bm, nj),
        in_specs=in_specs + c_in,
        out_specs=[pl.BlockSpec((bm, bn), o_map)] + c_out,
        out_shape=[jax.ShapeDtypeStruct((m, n), out_dtype)] + c_shapes,
        compiler_params=_params(2, blocks + c_bytes, 2 * _nbytes((bm, bn), F32)),
        name=name,
    )(*args, *[src for src, _ in casts])
    return out if casts else out[0]


def _matmul_ktiled_kernel(*refs, n_cast):
    a_ref, w_ref, add_ref = refs[:3]
    src_refs = refs[3:3 + n_cast]
    o_ref = refs[3 + n_cast]
    dst_refs = refs[4 + n_cast:]
    _run_side_casts(src_refs, dst_refs)

    @pl.when(pl.program_id(2) == 0)
    def _():
        o_ref[...] = add_ref[...]

    o_ref[...] += jnp.dot(a_ref[...], w_ref[...], preferred_element_type=F32)


def matmul_ktiled(a, w, add, *, casts=(), bm=1024, bn=512, bk=4096, name="matmul_ktiled"):
    m, kdim = a.shape
    n = w.shape[1]
    assert m % bm == 0 and n % bn == 0 and kdim % bk == 0 and add.dtype == F32
    nj, nk = n // bn, kdim // bk
    o_map = lambda i, j, k: (i, j)
    blocks = (_nbytes((bm, bk), a.dtype) + _nbytes((bk, bn), w.dtype)
              + 2 * _nbytes((bm, bn), F32))
    c_in, c_out, c_shapes, c_bytes = _side_cast_specs(
        casts, (m // bm) * nj * nk, lambda i, j, k: (i * nj + j) * nk + k)
    out = pl.pallas_call(
        functools.partial(_matmul_ktiled_kernel, n_cast=len(casts)),
        grid=(m // bm, nj, nk),
        in_specs=[pl.BlockSpec((bm, bk), lambda i, j, k: (i, k)),
                  pl.BlockSpec((bk, bn), lambda i, j, k: (k, j)),
                  pl.BlockSpec((bm, bn), o_map)] + c_in,
        out_specs=[pl.BlockSpec((bm, bn), o_map)] + c_out,
        out_shape=[jax.ShapeDtypeStruct((m, n), F32)] + c_shapes,
        compiler_params=_params(3, blocks + c_bytes, 2 * _nbytes((bm, bn), F32)),
        name=name,
    )(a, w, add, *[src for src, _ in casts])
    return out if casts else out[0]


def _cast_kernel(src_ref, dst_ref):
    dst_ref[...] = src_ref[0].astype(BF16)


def cast_layer(w, layer, rows=64):
    _, kdim, n = w.shape
    blocks = _nbytes((rows, n), F32) + _nbytes((rows, n), BF16)
    return pl.pallas_call(
        _cast_kernel,
        grid=(kdim // rows,),
        in_specs=[pl.BlockSpec((1, rows, n), lambda i: (layer, i, 0))],
        out_specs=pl.BlockSpec((rows, n), lambda i: (i, 0)),
        out_shape=jax.ShapeDtypeStruct((kdim, n), BF16),
        compiler_params=_params(1, blocks),
        name="cast_weights",
    )(w)


def _rope_table_kernel(pos_ref, freq_ref, cos_ref, sin_ref):
    ang = pos_ref[...].astype(F32) * freq_ref[...]
    lane = lax.broadcasted_iota(jnp.int32, ang.shape, 1)
    sign = jnp.where(lane < HEAD_DIM // 2, -1.0, 1.0)
    cos_ref[...] = jnp.cos(ang)
    sin_ref[...] = jnp.sin(ang) * sign


def rope_tables(positions, rows=512):
    t = positions.size
    inv_freq = ROPE_THETA ** (-jnp.arange(0, HEAD_DIM, 2, dtype=F32) / HEAD_DIM)
    freq = jnp.concatenate([inv_freq, inv_freq]).reshape(1, HEAD_DIM)
    tab = jax.ShapeDtypeStruct((t, HEAD_DIM), F32)
    return pl.pallas_call(
        _rope_table_kernel,
        grid=(t // rows,),
        in_specs=[pl.BlockSpec((rows, 1), lambda i: (i, 0)),
                  pl.BlockSpec((1, HEAD_DIM), lambda i: (0, 0))],
        out_specs=[pl.BlockSpec((rows, HEAD_DIM), lambda i: (i, 0))] * 2,
        out_shape=[tab, tab],
        compiler_params=_params(1, 3 * _nbytes((rows, HEAD_DIM), F32)),
        name="rope_tables",
    )(positions.reshape(t, 1), freq)


Q_PRESCALE = HEAD_DIM ** -0.5 * math.log2(math.e)


def _qk_prep_kernel(zq_ref, zk_ref, g_ref, cos_ref, sin_ref, q_ref, kt_ref):
    cos = cos_ref[...]
    sin = sin_ref[...]

    def norm_rope(x, g):
        ms = jnp.mean(x * x, axis=-1, keepdims=True)
        y = x * lax.rsqrt(ms + EPS) * g
        return y * cos + pltpu.roll(y, HEAD_DIM // 2, 1) * sin

    for f in range(QK_WIDTH // HEAD_DIM):
        sl = slice(f * HEAD_DIM, (f + 1) * HEAD_DIM)
        q = norm_rope(zq_ref[:, sl].astype(F32), g_ref[0])
        q_ref[:, sl] = (q * Q_PRESCALE).astype(q_ref.dtype)
        k = norm_rope(zk_ref[:, sl].astype(F32), g_ref[1])
        kt_ref[0, f, 0] = k.T.astype(kt_ref.dtype)


def qk_prep(z, qk_gain, cos, sin, *, batch, seq, blk):
    t = z.shape[0]
    nk = seq // blk
    nf = QK_WIDTH // HEAD_DIM
    blocks = (4 * _nbytes((blk, QK_WIDTH), BF16) + 2 * _nbytes((blk, HEAD_DIM), F32))
    return pl.pallas_call(
        _qk_prep_kernel,
        grid=(t // blk,),
        in_specs=[pl.BlockSpec((blk, QK_WIDTH), lambda i: (i, COL_Q // QK_WIDTH)),
                  pl.BlockSpec((blk, QK_WIDTH), lambda i: (i, COL_K // QK_WIDTH)),
                  pl.BlockSpec((2, 1, HEAD_DIM), lambda i: (0, 0, 0)),
                  pl.BlockSpec((blk, HEAD_DIM), lambda i: (i, 0)),
                  pl.BlockSpec((blk, HEAD_DIM), lambda i: (i, 0))],
        out_specs=[pl.BlockSpec((blk, QK_WIDTH), lambda i: (i, 0)),
                   pl.BlockSpec((1, nf, 1, HEAD_DIM, blk), lambda i: (i // nk, 0, i % nk, 0, 0))],
        out_shape=[jax.ShapeDtypeStruct((t, QK_WIDTH), BF16),
                   jax.ShapeDtypeStruct((batch, nf, nk, HEAD_DIM, blk), BF16)],
        compiler_params=_params(1, blocks, 8 * _nbytes((blk, HEAD_DIM), F32)),
        name="qk_prep",
    )(z, z, qk_gain, cos, sin)


def _gmlp_kernel(z_ref, w_ref, b_ref, g_ref, beta_ref, o_ref, vn_ref, *, rows):
    v = jax.nn.gelu(z_ref[:, GM_WIDTH:].astype(F32))
    mu = jnp.mean(v, axis=-1, keepdims=True)
    vc = v - mu
    var = jnp.mean(vc * vc, axis=-1, keepdims=True)
    vn = vc * lax.rsqrt(var + EPS) * g_ref[...] + beta_ref[...]
    vn_ref[...] = vn.astype(BF16)
    t_idx = lax.broadcasted_iota(jnp.int32, (CHUNK, CHUNK), 0)
    s_idx = lax.broadcasted_iota(jnp.int32, (CHUNK, CHUNK), 1)
    causal = s_idx <= t_idx
    for g in range(GM_GROUPS):
        w = jnp.where(causal, w_ref[g], 0.0).astype(BF16)
        bias = b_ref[:, g:g + 1]
        cols = slice(g * GM_GROUP_DIM, (g + 1) * GM_GROUP_DIM)
        for c in range(rows // CHUNK):
            rs = slice(c * CHUNK, (c + 1) * CHUNK)
            mixed = jnp.dot(w, vn_ref[rs, cols], preferred_element_type=F32) + bias
            u = jax.nn.gelu(z_ref[rs, cols].astype(F32))
            o_ref[rs, cols] = (u * mixed).astype(o_ref.dtype)


def gmlp_gating(z, w_s, b_s, ln_g, ln_b, rows=256):
    t = z.shape[0]
    blocks = _nbytes((rows, 2 * GM_WIDTH), BF16) + _nbytes((rows, GM_WIDTH), BF16)
    small = _nbytes(w_s.shape, F32) + _nbytes((CHUNK, 128), F32) + 2 * _nbytes((8, GM_WIDTH), F32)
    return pl.pallas_call(
        functools.partial(_gmlp_kernel, rows=rows),
        grid=(t // rows,),
        in_specs=[pl.BlockSpec((rows, 2 * GM_WIDTH), lambda i: (i, COL_U)),
                  pl.BlockSpec(w_s.shape, lambda i: (0, 0, 0)),
                  pl.BlockSpec((CHUNK, GM_GROUPS), lambda i: (0, 0)),
                  pl.BlockSpec((1, GM_WIDTH), lambda i: (0, 0)),
                  pl.BlockSpec((1, GM_WIDTH), lambda i: (0, 0))],
        out_specs=pl.BlockSpec((rows, GM_WIDTH), lambda i: (i, 0)),
        out_shape=jax.ShapeDtypeStruct((t, GM_WIDTH), BF16),
        scratch_shapes=[pltpu.VMEM((rows, GM_WIDTH), BF16)],
        compiler_params=_params(1, blocks + small, 4 * _nbytes((rows, GM_WIDTH), F32)),
        name="gmlp_gating",
    )(z, w_s, b_s.T, ln_g.reshape(1, GM_WIDTH), ln_b.reshape(1, GM_WIDTH))


LANES = 128
ATTN_BQ = 2048
ATTN_BK = 512
ATTN_RQ = 128
ATTN_UNROLL = 4


def _attn_kernel(lam_ref, q_ref, kt_ref, v_ref, g_ref, o_ref, m_ref, l_ref, acc_ref,
                 *, lambda_init):
    bq, bk, rq = ATTN_BQ, ATTN_BK, ATTN_RQ
    i = pl.program_id(2)
    m_ref[...] = jnp.full(m_ref.shape, NEG_INF, F32)
    l_ref[...] = jnp.zeros(l_ref.shape, F32)
    acc_ref[...] = jnp.zeros(acc_ref.shape, F32)

    def chain(j, c, r, masked):
        kv_rows = pl.ds(pl.multiple_of(j * bk, bk), bk)
        rs = slice(r * rq, (r + 1) * rq)
        q = q_ref[rs, c * HEAD_DIM:(c + 1) * HEAD_DIM]
        s = jnp.dot(q, kt_ref[0, c, j], preferred_element_type=F32)
        if masked:
            qi = lax.broadcasted_iota(jnp.int32, s.shape, 0) + (r * rq) % bk
            ki = lax.broadcasted_iota(jnp.int32, s.shape, 1)
            s = jnp.where(ki <= qi, s, NEG_INF)
        m_prev = m_ref[c, rs, :]
        m_new = jnp.maximum(m_prev, jnp.max(s, axis=-1, keepdims=True))
        alpha = jnp.exp2(m_prev - m_new)
        parts = [jnp.exp2(s[:, t * LANES:(t + 1) * LANES] - m_new) for t in range(bk // LANES)]
        l_ref[c, rs, :] = alpha * l_ref[c, rs, :] + functools.reduce(jnp.add, parts)
        p = jnp.concatenate(parts, axis=-1).astype(BF16)
        pv = jnp.dot(p, v_ref[kv_rows, :], preferred_element_type=F32)
        alpha_v = jnp.concatenate([alpha] * (V_DIM // LANES), axis=-1)
        acc_ref[c, rs, :] = alpha_v * acc_ref[c, rs, :] + pv
        m_ref[c, rs, :] = m_new

    n_diag = bq // bk
    n_full = i * n_diag

    def body(jj, carry):
        for u in range(ATTN_UNROLL):
            for r in range(bq // rq):
                for c in range(2):
                    chain(jj * ATTN_UNROLL + u, c, r, False)
        return carry

    lax.fori_loop(0, n_full // ATTN_UNROLL, body, 0)

    for d in range(n_diag):
        for r in range(bq // rq):
            row_lo, row_hi = r * rq, (r + 1) * rq - 1
            col_lo, col_hi = d * bk, (d + 1) * bk - 1
            if row_hi < col_lo:
                continue
            for c in range(2):
                chain(n_full + d, c, r, col_hi > row_lo)

    lp = lam_ref[...]
    lam = (jnp.exp(jnp.sum(lp[0:1] * lp[1:2], axis=-1, keepdims=True))
           - jnp.exp(jnp.sum(lp[2:3] * lp[3:4], axis=-1, keepdims=True))
           + lambda_init)
    l0 = jnp.sum(l_ref[0], axis=-1, keepdims=True)
    l1 = jnp.sum(l_ref[1], axis=-1, keepdims=True)
    o = acc_ref[0] / l0 - lam * (acc_ref[1] / l1)
    ms = jnp.mean(o * o, axis=-1, keepdims=True)
    o = o * lax.rsqrt(ms + EPS) * g_ref[...]
    o_ref[...] = (o * (1.0 - lambda_init)).astype(o_ref.dtype)


def diff_attention(q, kt, z, lam_params, subln_g, *, batch, seq, lambda_init):
    bq, bk = ATTN_BQ, ATTN_BK
    assert bk % ATTN_RQ == 0 and bq % bk == 0
    assert (bq // bk) % ATTN_UNROLL == 0 and seq % bq == 0
    t = batch * seq
    nq = seq // bq
    nk = seq // bk
    blocks = (2 * _nbytes((bq, V_DIM), BF16) + 2 * _nbytes((seq, V_DIM), BF16))
    scratch = 2 * _nbytes((bq, V_DIM), F32) + 4 * _nbytes((bq, LANES), F32)
    kern = functools.partial(_attn_kernel, lambda_init=lambda_init)
    return pl.pallas_call(
        kern,
        grid=(batch, HEADS, nq),
        in_specs=[pl.BlockSpec((4, HEAD_DIM), lambda b, h, i: (0, 0)),
                  pl.BlockSpec((bq, V_DIM), lambda b, h, i: (b * nq + i, h)),
                  pl.BlockSpec((1, 2, nk, HEAD_DIM, bk), lambda b, h, i: (b, h, 0, 0, 0)),
                  pl.BlockSpec((seq, V_DIM), lambda b, h, i: (b, COL_VA // V_DIM + h)),
                  pl.BlockSpec((1, V_DIM), lambda b, h, i: (0, 0))],
        out_specs=pl.BlockSpec((bq, V_DIM), lambda b, h, i: (b * nq + i, h)),
        out_shape=jax.ShapeDtypeStruct((t, DA_WIDTH), BF16),
        scratch_shapes=[pltpu.VMEM((2, bq, LANES), F32),
                        pltpu.VMEM((2, bq, LANES), F32),
                        pltpu.VMEM((2, bq, V_DIM), F32)],
        compiler_params=_params(3, blocks, 2 * scratch),
        name="diff_attention",
    )(lam_params, q, kt, z, subln_g.reshape(1, V_DIM))


def kernel(x, positions, mix_norm_g, w_in, gm_w_s, gm_b_s, gm_ln_g, gm_ln_b, q_norm_g, k_norm_g,
           lambda_q1, lambda_k1, lambda_q2, lambda_k2, subln_g, w_proj_a, w_proj_b, w_out,
           mlp_norm_g, w_up, w_down):
    batch, seq, d = x.shape
    depth = w_in.shape[0]
    t = batch * seq
    xf = x.reshape(t, d)
    cos, sin = rope_tables(positions)
    w_in_bf = cast_layer(w_in, 0)
    for l in range(depth):
        lambda_init = 0.8 - 0.6 * math.exp(-0.3 * l)
        h = rmsnorm(xf, mix_norm_g[l])
        z, w_up_bf, w_pa_bf = matmul(h, w_in_bf, out_dtype=BF16,
                                     casts=[(w_up, l), (w_proj_a, l)], name="in_proj")
        ya = gmlp_gating(z, gm_w_s[l], gm_b_s[l], gm_ln_g[l], gm_ln_b[l])
        qk_gain = jnp.stack([q_norm_g[l], k_norm_g[l]]).reshape(2, 1, HEAD_DIM)
        q, kt = qk_prep(z, qk_gain, cos, sin, batch=batch, seq=seq, blk=ATTN_BK)
        lam_params = jnp.stack([lambda_q1[l], lambda_k1[l], lambda_q2[l], lambda_k2[l]])
        o = diff_attention(q, kt, z, lam_params, subln_g[l], batch=batch, seq=seq,
                           lambda_init=lambda_init)
        ga, w_pb_bf = matmul(ya, w_pa_bf, out_dtype=F32, gate=z, gate_col=COL_GA,
                             casts=[(w_proj_b, l)], name="proj_a_gate")
        merged, w_out_bf = matmul(o, w_pb_bf, out_dtype=BF16, gate=z, gate_col=COL_GB, add=ga,
                                  casts=[(w_out, l)], bn=512, name="proj_b_gate_merge")
        xf = matmul(merged, w_out_bf, out_dtype=F32, add=xf, bn=512, name="out_proj")
        h = rmsnorm(xf, mlp_norm_g[l])
        hid, w_down_bf = matmul(h, w_up_bf, out_dtype=BF16, act="relu2",
                                casts=[(w_down, l)], name="mlp_up")
        if l + 1 < depth:
            xf, w_in_bf = matmul_ktiled(hid, w_down_bf, xf, casts=[(w_in, l + 1)],
                                        name="mlp_down")
        else:
            xf = matmul_ktiled(hid, w_down_bf, xf, name="mlp_down")
    return xf.reshape(batch, seq, d)
```

```python
import functools
import math

import jax
import jax.numpy as jnp
from jax import lax
from jax.experimental import pallas as pl
from jax.experimental.pallas import tpu as pltpu

D_MODEL = 4096
CHUNK = 128
GM_WIDTH = D_MODEL // 2
GM_GROUPS = 8
GM_GROUP_DIM = GM_WIDTH // GM_GROUPS
HEAD_DIM = 128
V_DIM = 2 * HEAD_DIM
HEADS = D_MODEL // V_DIM
QK_WIDTH = HEADS * 2 * HEAD_DIM
DA_WIDTH = HEADS * V_DIM
D_FF = 4 * D_MODEL
ROPE_THETA = 10000.0
EPS = 1e-6
NEG_INF = -1e30
COL_U = 0
COL_Q = 2 * GM_WIDTH
COL_K = COL_Q + QK_WIDTH
COL_VA = COL_K + QK_WIDTH
COL_GA = COL_VA + DA_WIDTH
COL_GB = COL_GA + D_MODEL
IN_COLS = COL_GB + D_MODEL

V7X_VMEM_BYTES = 64 * 1024 * 1024
VMEM_CAP_BYTES = V7X_VMEM_BYTES - 8 * 1024 * 1024

F32 = jnp.float32
BF16 = jnp.bfloat16


def _nbytes(shape, dtype):
    return math.prod(shape) * jnp.dtype(dtype).itemsize


def _params(n_grid, block_bytes, scratch_bytes=0):
    need = 2 * block_bytes + scratch_bytes
    limit = min(VMEM_CAP_BYTES, max(need + need // 4, 32 * 1024 * 1024))
    return pltpu.CompilerParams(
        dimension_semantics=("arbitrary",) * n_grid, vmem_limit_bytes=limit)


def _rmsnorm_kernel(x_ref, g_ref, o_ref):
    x = x_ref[...]
    ms = jnp.mean(x * x, axis=-1, keepdims=True)
    o_ref[...] = (x * lax.rsqrt(ms + EPS) * g_ref[...]).astype(o_ref.dtype)


def rmsnorm(x, g, rows=512):
    t, d = x.shape
    blocks = _nbytes((rows, d), F32) + _nbytes((rows, d), BF16)
    return pl.pallas_call(
        _rmsnorm_kernel,
        grid=(t // rows,),
        in_specs=[pl.BlockSpec((rows, d), lambda i: (i, 0)),
                  pl.BlockSpec((1, d), lambda i: (0, 0))],
        out_specs=pl.BlockSpec((rows, d), lambda i: (i, 0)),
        out_shape=jax.ShapeDtypeStruct((t, d), BF16),
        compiler_params=_params(1, blocks, _nbytes((rows, d), F32)),
        name="rmsnorm",
    )(x, g.reshape(1, d))


def _cast_block_rows(rows, n_steps):
    rb = 16
    while rows // rb > n_steps:
        rb *= 2
    assert rows % rb == 0
    return rb


def _side_cast_specs(casts, n_steps, step_index):
    in_specs, out_specs, out_shapes, nbytes = [], [], [], 0
    for src, layer in casts:
        _, rows, cols = src.shape
        rb = _cast_block_rows(rows, n_steps)
        last = rows // rb - 1

        def in_map(*g, layer=layer, last=last):
            return (layer, jnp.minimum(step_index(*g), last), 0)

        def out_map(*g, last=last):
            return (jnp.minimum(step_index(*g), last), 0)

        in_specs.append(pl.BlockSpec((1, rb, cols), in_map))
        out_specs.append(pl.BlockSpec((rb, cols), out_map))
        out_shapes.append(jax.ShapeDtypeStruct((rows, cols), BF16))
        nbytes += _nbytes((rb, cols), F32) + _nbytes((rb, cols), BF16)
    return in_specs, out_specs, out_shapes, nbytes


def _run_side_casts(src_refs, dst_refs):
    for src, dst in zip(src_refs, dst_refs):
        dst[...] = src[0].astype(BF16)


Q_PRESCALE = HEAD_DIM ** -0.5 * math.log2(math.e)
ROPE_ROWS = 512


def _head_epilogue(r, gain_ref, cos_t_ref, sin_t_ref, o_ref, *, kind):
    half = HEAD_DIM // 2
    gain = gain_ref[...]
    for f in range(r.shape[1] // HEAD_DIM):
        cols = slice(f * HEAD_DIM, (f + 1) * HEAD_DIM)
        for t in range(r.shape[0] // ROPE_ROWS):
            rows = slice(t * ROPE_ROWS, (t + 1) * ROPE_ROWS)
            xt = r[rows, cols].T
            yt = xt * lax.rsqrt(jnp.mean(xt * xt, axis=0, keepdims=True) + EPS) * gain
            swapped = jnp.concatenate([yt[half:], yt[:half]], axis=0)
            out_t = yt * cos_t_ref[t] + swapped * sin_t_ref[t]
            if kind == "q":
                o_ref[rows, cols] = (out_t * Q_PRESCALE).T.astype(o_ref.dtype)
            else:
                o_ref[0, f, t] = out_t.astype(o_ref.dtype)


def _matmul_kernel(*refs, act, epilogue, has_gate, has_add, n_cast):
    a_ref, w_ref = refs[0], refs[1]
    pos = 2
    gate_ref = add_ref = None
    if has_gate:
        gate_ref = refs[pos]
        pos += 1
    if has_add:
        add_ref = refs[pos]
        pos += 1
    head_refs = refs[pos:pos + 3] if epilogue else ()
    pos += len(head_refs)
    src_refs = refs[pos:pos + n_cast]
    o_ref = refs[pos + n_cast]
    dst_refs = refs[pos + n_cast + 1:]
    _run_side_casts(src_refs, dst_refs)
    r = jnp.dot(a_ref[...], w_ref[...], preferred_element_type=F32)
    if epilogue:
        _head_epilogue(r, *head_refs, o_ref, kind=epilogue)
        return
    if act == "relu2":
        r = jnp.square(jnp.maximum(r, 0.0))
    if has_gate:
        r = r * jax.nn.sigmoid(gate_ref[...].astype(F32))
    if has_add:
        r = r + add_ref[...].astype(F32)
    o_ref[...] = r.astype(o_ref.dtype)


def matmul(a, w, *, out_dtype, n=None, w_col=0, act=None, gate=None, gate_col=0, add=None,
           epilogue=None, head=None, seq=None, casts=(), bm=1024, bn=1024, name="matmul"):
    m, kdim = a.shape
    n = w.shape[1] if n is None else n
    assert m % bm == 0 and n % bn == 0 and gate_col % bn == 0 and w_col % bn == 0
    nj = n // bn
    o_map = lambda i, j: (i, j)
    in_specs = [pl.BlockSpec((bm, kdim), lambda i, j: (i, 0)),
                pl.BlockSpec((kdim, bn), lambda i, j: (0, w_col // bn + j))]
    args = [a, w]
    blocks = (_nbytes((bm, kdim), a.dtype) + _nbytes((kdim, bn), w.dtype)
              + _nbytes((bm, bn), out_dtype))
    if gate is not None:
        in_specs.append(pl.BlockSpec((bm, bn), lambda i, j: (i, gate_col // bn + j)))
        args.append(gate)
        blocks += _nbytes((bm, bn), gate.dtype)
    if add is not None:
        in_specs.append(pl.BlockSpec((bm, bn), o_map))
        args.append(add)
        blocks += _nbytes((bm, bn), add.dtype)
    out_spec = pl.BlockSpec((bm, bn), o_map)
    out_shape = jax.ShapeDtypeStruct((m, n), out_dtype)
    if epilogue:
        gain, cos_t, sin_t = head
        assert bm % ROPE_ROWS == 0
        tb = bm // ROPE_ROWS
        tab_spec = pl.BlockSpec((tb, HEAD_DIM, ROPE_ROWS), lambda i, j: (i, 0, 0))
        in_specs += [pl.BlockSpec((HEAD_DIM, 1), lambda i, j: (0, 0)), tab_spec, tab_spec]
        args += [gain.reshape(HEAD_DIM, 1), cos_t, sin_t]
        blocks += 2 * _nbytes((bm, HEAD_DIM), F32)
    if epilogue == "k":
        assert seq % bm == 0
        per_seq = seq // bm
        out_spec = pl.BlockSpec((1, bn // HEAD_DIM, tb, HEAD_DIM, ROPE_ROWS),
                                lambda i, j: (i // per_seq, j, i % per_seq, 0, 0))
        out_shape = jax.ShapeDtypeStruct(
            (m // seq, n // HEAD_DIM, seq // ROPE_ROWS, HEAD_DIM, ROPE_ROWS), out_dtype)
    c_in, c_out, c_shapes, c_bytes = _side_cast_specs(
        casts, (m // bm) * nj, lambda i, j: i * nj + j)
    kern = functools.partial(_matmul_kernel, act=act, epilogue=epilogue,
                             has_gate=gate is not None, has_add=add is not None,
                             n_cast=len(casts))
    out = pl.pallas_call(
        kern,
        grid=(m // bm, nj),
        in_specs=in_specs + c_in,
        out_specs=[out_spec] + c_out,
        out_shape=[out_shape] + c_shapes,
        compiler_params=_params(2, blocks + c_bytes, 2 * _nbytes((bm, bn), F32)),
        name=name,
    )(*args, *[src for src, _ in casts])
    return out if casts else out[0]


def _matmul_ktiled_kernel(*refs, n_cast):
    a_ref, w_ref, add_ref = refs[:3]
    src_refs = refs[3:3 + n_cast]
    o_ref = refs[3 + n_cast]
    dst_refs = refs[4 + n_cast:]
    _run_side_casts(src_refs, dst_refs)

    @pl.when(pl.program_id(2) == 0)
    def _():
        o_ref[...] = add_ref[...]

    o_ref[...] += jnp.dot(a_ref[...], w_ref[...], preferred_element_type=F32)


def matmul_ktiled(a, w, add, *, casts=(), bm=1024, bn=512, bk=4096, name="matmul_ktiled"):
    m, kdim = a.shape
    n = w.shape[1]
    assert m % bm == 0 and n % bn == 0 and kdim % bk == 0 and add.dtype == F32
    nj, nk = n // bn, kdim // bk
    o_map = lambda i, j, k: (i, j)
    blocks = (_nbytes((bm, bk), a.dtype) + _nbytes((bk, bn), w.dtype)
              + 2 * _nbytes((bm, bn), F32))
    c_in, c_out, c_shapes, c_bytes = _side_cast_specs(
        casts, (m // bm) * nj * nk, lambda i, j, k: (i * nj + j) * nk + k)
    out = pl.pallas_call(
        functools.partial(_matmul_ktiled_kernel, n_cast=len(casts)),
        grid=(m // bm, nj, nk),
        in_specs=[pl.BlockSpec((bm, bk), lambda i, j, k: (i, k)),
                  pl.BlockSpec((bk, bn), lambda i, j, k: (k, j)),
                  pl.BlockSpec((bm, bn), o_map)] + c_in,
        out_specs=[pl.BlockSpec((bm, bn), o_map)] + c_out,
        out_shape=[jax.ShapeDtypeStruct((m, n), F32)] + c_shapes,
        compiler_params=_params(3, blocks + c_bytes, 2 * _nbytes((bm, bn), F32)),
        name=name,
    )(a, w, add, *[src for src, _ in casts])
    return out if casts else out[0]


def _cast_kernel(src_ref, dst_ref):
    dst_ref[...] = src_ref[0].astype(BF16)


def cast_layer(w, layer, rows=64):
    _, kdim, n = w.shape
    blocks = _nbytes((rows, n), F32) + _nbytes((rows, n), BF16)
    return pl.pallas_call(
        _cast_kernel,
        grid=(kdim // rows,),
        in_specs=[pl.BlockSpec((1, rows, n), lambda i: (layer, i, 0))],
        out_specs=pl.BlockSpec((rows, n), lambda i: (i, 0)),
        out_shape=jax.ShapeDtypeStruct((kdim, n), BF16),
        compiler_params=_params(1, blocks),
        name="cast_weights",
    )(w)


def _rope_table_kernel(pos_ref, freq_ref, cos_t_ref, sin_t_ref):
    ang_t = freq_ref[...] * pos_ref[0].astype(F32)
    sub = lax.broadcasted_iota(jnp.int32, ang_t.shape, 0)
    cos_t_ref[0] = jnp.cos(ang_t)
    sin_t_ref[0] = jnp.sin(ang_t) * jnp.where(sub < HEAD_DIM // 2, -1.0, 1.0)


def rope_tables(positions):
    nb = positions.size // ROPE_ROWS
    inv_freq = ROPE_THETA ** (-jnp.arange(0, HEAD_DIM, 2, dtype=F32) / HEAD_DIM)
    freq = jnp.concatenate([inv_freq, inv_freq]).reshape(HEAD_DIM, 1)
    tab = jax.ShapeDtypeStruct((nb, HEAD_DIM, ROPE_ROWS), F32)
    return pl.pallas_call(
        _rope_table_kernel,
        grid=(nb,),
        in_specs=[pl.BlockSpec((1, 1, ROPE_ROWS), lambda i: (i, 0, 0)),
                  pl.BlockSpec((HEAD_DIM, 1), lambda i: (0, 0))],
        out_specs=[pl.BlockSpec((1, HEAD_DIM, ROPE_ROWS), lambda i: (i, 0, 0))] * 2,
        out_shape=[tab, tab],
        compiler_params=_params(1, 3 * _nbytes((ROPE_ROWS, HEAD_DIM), F32)),
        name="rope_tables",
    )(positions.reshape(nb, 1, ROPE_ROWS), freq)


def _gmlp_kernel(z_ref, w_ref, b_ref, g_ref, beta_ref, o_ref, vn_ref, *, rows):
    v = jax.nn.gelu(z_ref[:, GM_WIDTH:].astype(F32))
    mu = jnp.mean(v, axis=-1, keepdims=True)
    vc = v - mu
    var = jnp.mean(vc * vc, axis=-1, keepdims=True)
    vn = vc * lax.rsqrt(var + EPS) * g_ref[...] + beta_ref[...]
    vn_ref[...] = vn.astype(BF16)
    t_idx = lax.broadcasted_iota(jnp.int32, (CHUNK, CHUNK), 0)
    s_idx = lax.broadcasted_iota(jnp.int32, (CHUNK, CHUNK), 1)
    causal = s_idx <= t_idx
    for g in range(GM_GROUPS):
        w = jnp.where(causal, w_ref[g], 0.0).astype(BF16)
        bias = b_ref[:, g:g + 1]
        cols = slice(g * GM_GROUP_DIM, (g + 1) * GM_GROUP_DIM)
        for c in range(rows // CHUNK):
            rs = slice(c * CHUNK, (c + 1) * CHUNK)
            mixed = jnp.dot(w, vn_ref[rs, cols], preferred_element_type=F32) + bias
            u = jax.nn.gelu(z_ref[rs, cols].astype(F32))
            o_ref[rs, cols] = (u * mixed).astype(o_ref.dtype)


def gmlp_gating(z, w_s, b_s, ln_g, ln_b, rows=256):
    t = z.shape[0]
    blocks = _nbytes((rows, 2 * GM_WIDTH), BF16) + _nbytes((rows, GM_WIDTH), BF16)
    small = _nbytes(w_s.shape, F32) + _nbytes((CHUNK, 128), F32) + 2 * _nbytes((8, GM_WIDTH), F32)
    return pl.pallas_call(
        functools.partial(_gmlp_kernel, rows=rows),
        grid=(t // rows,),
        in_specs=[pl.BlockSpec((rows, 2 * GM_WIDTH), lambda i: (i, COL_U)),
                  pl.BlockSpec(w_s.shape, lambda i: (0, 0, 0)),
                  pl.BlockSpec((CHUNK, GM_GROUPS), lambda i: (0, 0)),
                  pl.BlockSpec((1, GM_WIDTH), lambda i: (0, 0)),
                  pl.BlockSpec((1, GM_WIDTH), lambda i: (0, 0))],
        out_specs=pl.BlockSpec((rows, GM_WIDTH), lambda i: (i, 0)),
        out_shape=jax.ShapeDtypeStruct((t, GM_WIDTH), BF16),
        scratch_shapes=[pltpu.VMEM((rows, GM_WIDTH), BF16)],
        compiler_params=_params(1, blocks + small, 4 * _nbytes((rows, GM_WIDTH), F32)),
        name="gmlp_gating",
    )(z, w_s, b_s.T, ln_g.reshape(1, GM_WIDTH), ln_b.reshape(1, GM_WIDTH))


LANES = 128
ATTN_BQ = 2048
ATTN_BK = ROPE_ROWS
ATTN_RQ = 128
ATTN_UNROLL = 4


def _attn_kernel(lam_ref, q_ref, kt_ref, v_ref, g_ref, o_ref, m_ref, l_ref, acc_ref,
                 *, lambda_init):
    bq, bk, rq = ATTN_BQ, ATTN_BK, ATTN_RQ
    i = pl.program_id(2)
    m_ref[...] = jnp.full(m_ref.shape, NEG_INF, F32)
    l_ref[...] = jnp.zeros(l_ref.shape, F32)
    acc_ref[...] = jnp.zeros(acc_ref.shape, F32)

    def chain(j, c, r, masked):
        kv_rows = pl.ds(pl.multiple_of(j * bk, bk), bk)
        rs = slice(r * rq, (r + 1) * rq)
        q = q_ref[rs, c * HEAD_DIM:(c + 1) * HEAD_DIM]
        s = jnp.dot(q, kt_ref[0, c, j], preferred_element_type=F32)
        if masked:
            qi = lax.broadcasted_iota(jnp.int32, s.shape, 0) + (r * rq) % bk
            ki = lax.broadcasted_iota(jnp.int32, s.shape, 1)
            s = jnp.where(ki <= qi, s, NEG_INF)
        m_prev = m_ref[c, rs, :]
        m_new = jnp.maximum(m_prev, jnp.max(s, axis=-1, keepdims=True))
        alpha = jnp.exp2(m_prev - m_new)
        parts = [jnp.exp2(s[:, t * LANES:(t + 1) * LANES] - m_new) for t in range(bk // LANES)]
        l_ref[c, rs, :] = alpha * l_ref[c, rs, :] + functools.reduce(jnp.add, parts)
        p = jnp.concatenate(parts, axis=-1).astype(BF16)
        pv = jnp.dot(p, v_ref[kv_rows, :], preferred_element_type=F32)
        alpha_v = jnp.concatenate([alpha] * (V_DIM // LANES), axis=-1)
        acc_ref[c, rs, :] = alpha_v * acc_ref[c, rs, :] + pv
        m_ref[c, rs, :] = m_new

    n_diag = bq // bk
    n_full = i * n_diag

    def body(jj, carry):
        for u in range(ATTN_UNROLL):
            for r in range(bq // rq):
                for c in range(2):
                    chain(jj * ATTN_UNROLL + u, c, r, False)
        return carry

    lax.fori_loop(0, n_full // ATTN_UNROLL, body, 0)

    for d in range(n_diag):
        for r in range(bq // rq):
            row_lo, row_hi = r * rq, (r + 1) * rq - 1
            col_lo, col_hi = d * bk, (d + 1) * bk - 1
            if row_hi < col_lo:
                continue
            for c in range(2):
                chain(n_full + d, c, r, col_hi > row_lo)

    lp = lam_ref[...]
    lam = (jnp.exp(jnp.sum(lp[0:1] * lp[1:2], axis=-1, keepdims=True))
           - jnp.exp(jnp.sum(lp[2:3] * lp[3:4], axis=-1, keepdims=True))
           + lambda_init)
    l0 = jnp.sum(l_ref[0], axis=-1, keepdims=True)
    l1 = jnp.sum(l_ref[1], axis=-1, keepdims=True)
    o = acc_ref[0] / l0 - lam * (acc_ref[1] / l1)
    ms = jnp.mean(o * o, axis=-1, keepdims=True)
    o = o * lax.rsqrt(ms + EPS) * g_ref[...]
    o_ref[...] = (o * (1.0 - lambda_init)).astype(o_ref.dtype)


def diff_attention(q, kt, va, lam_params, subln_g, *, batch, seq, lambda_init):
    bq, bk = ATTN_BQ, ATTN_BK
    assert bk % ATTN_RQ == 0 and bq % bk == 0
    assert (bq // bk) % ATTN_UNROLL == 0 and seq % bq == 0
    t = batch * seq
    nq = seq // bq
    nk = seq // bk
    blocks = (2 * _nbytes((bq, V_DIM), BF16) + 2 * _nbytes((seq, V_DIM), BF16))
    scratch = 2 * _nbytes((bq, V_DIM), F32) + 4 * _nbytes((bq, LANES), F32)
    kern = functools.partial(_attn_kernel, lambda_init=lambda_init)
    return pl.pallas_call(
        kern,
        grid=(batch, HEADS, nq),
        in_specs=[pl.BlockSpec((4, HEAD_DIM), lambda b, h, i: (0, 0)),
                  pl.BlockSpec((bq, V_DIM), lambda b, h, i: (b * nq + i, h)),
                  pl.BlockSpec((1, 2, nk, HEAD_DIM, bk), lambda b, h, i: (b, h, 0, 0, 0)),
                  pl.BlockSpec((seq, V_DIM), lambda b, h, i: (b, h)),
                  pl.BlockSpec((1, V_DIM), lambda b, h, i: (0, 0))],
        out_specs=pl.BlockSpec((bq, V_DIM), lambda b, h, i: (b * nq + i, h)),
        out_shape=jax.ShapeDtypeStruct((t, DA_WIDTH), BF16),
        scratch_shapes=[pltpu.VMEM((2, bq, LANES), F32),
                        pltpu.VMEM((2, bq, LANES), F32),
                        pltpu.VMEM((2, bq, V_DIM), F32)],
        compiler_params=_params(3, blocks, 2 * scratch),
        name="diff_attention",
    )(lam_params, q, kt, va, subln_g.reshape(1, V_DIM))


def kernel(x, positions, mix_norm_g, w_in, gm_w_s, gm_b_s, gm_ln_g, gm_ln_b, q_norm_g, k_norm_g,
           lambda_q1, lambda_k1, lambda_q2, lambda_k2, subln_g, w_proj_a, w_proj_b, w_out,
           mlp_norm_g, w_up, w_down):
    batch, seq, d = x.shape
    depth = w_in.shape[0]
    t = batch * seq
    xf = x.reshape(t, d)
    cos_t, sin_t = rope_tables(positions)
    w_in_bf = cast_layer(w_in, 0)
    for l in range(depth):
        lambda_init = 0.8 - 0.6 * math.exp(-0.3 * l)
        h = rmsnorm(xf, mix_norm_g[l])
        z_uv = matmul(h, w_in_bf, out_dtype=BF16, n=2 * GM_WIDTH, w_col=COL_U, name="in_proj_uv")
        q = matmul(h, w_in_bf, out_dtype=BF16, n=QK_WIDTH, w_col=COL_Q, epilogue="q",
                   head=(q_norm_g[l], cos_t, sin_t), name="in_proj_q")
        kt = matmul(h, w_in_bf, out_dtype=BF16, n=QK_WIDTH, w_col=COL_K, epilogue="k",
                    head=(k_norm_g[l], cos_t, sin_t), seq=seq, name="in_proj_k")
        z_tail, w_up_bf, w_pa_bf = matmul(h, w_in_bf, out_dtype=BF16, n=IN_COLS - COL_VA,
                                          w_col=COL_VA, casts=[(w_up, l), (w_proj_a, l)],
                                          name="in_proj_tail")
        ya = gmlp_gating(z_uv, gm_w_s[l], gm_b_s[l], gm_ln_g[l], gm_ln_b[l])
        lam_params = jnp.stack([lambda_q1[l], lambda_k1[l], lambda_q2[l], lambda_k2[l]])
        o = diff_attention(q, kt, z_tail, lam_params, subln_g[l], batch=batch, seq=seq,
                           lambda_init=lambda_init)
        ga, w_pb_bf = matmul(ya, w_pa_bf, out_dtype=F32, gate=z_tail, gate_col=COL_GA - COL_VA,
                             casts=[(w_proj_b, l)], name="proj_a_gate")
        merged, w_out_bf = matmul(o, w_pb_bf, out_dtype=BF16, gate=z_tail,
                                  gate_col=COL_GB - COL_VA, add=ga,
                                  casts=[(w_out, l)], bn=512, name="proj_b_gate_merge")
        xf = matmul(merged, w_out_bf, out_dtype=F32, add=xf, bn=512, name="out_proj")
        h = rmsnorm(xf, mlp_norm_g[l])
        hid, w_down_bf = matmul(h, w_up_bf, out_dtype=BF16, act="relu2",
                                casts=[(w_down, l)], name="mlp_up")
        if l + 1 < depth:
            xf, w_in_bf = matmul_ktiled(hid, w_down_bf, xf, casts=[(w_in, l + 1)],
                                        name="mlp_down")
        else:
            xf = matmul_ktiled(hid, w_down_bf, xf, name="mlp_down")
    return xf.reshape(batch, seq, d)
```

```python
import functools
import math

import jax
import jax.numpy as jnp
from jax import lax
from jax.experimental import pallas as pl
from jax.experimental.pallas import tpu as pltpu

D_MODEL = 4096
CHUNK = 128
GM_WIDTH = D_MODEL // 2
GM_GROUPS = 8
GM_GROUP_DIM = GM_WIDTH // GM_GROUPS
HEAD_DIM = 128
V_DIM = 2 * HEAD_DIM
HEADS = D_MODEL // V_DIM
QK_WIDTH = HEADS * 2 * HEAD_DIM
DA_WIDTH = HEADS * V_DIM
D_FF = 4 * D_MODEL
ROPE_THETA = 10000.0
EPS = 1e-6
NEG_INF = -1e30
COL_U = 0
COL_Q = 2 * GM_WIDTH
COL_K = COL_Q + QK_WIDTH
COL_VA = COL_K + QK_WIDTH
COL_GA = COL_VA + DA_WIDTH
COL_GB = COL_GA + D_MODEL
IN_COLS = COL_GB + D_MODEL

V7X_VMEM_BYTES = 64 * 1024 * 1024
VMEM_CAP_BYTES = V7X_VMEM_BYTES - 8 * 1024 * 1024

F32 = jnp.float32
BF16 = jnp.bfloat16
LANES = 128


def _nbytes(shape, dtype):
    return math.prod(shape) * jnp.dtype(dtype).itemsize


def _params(n_grid, block_bytes, scratch_bytes=0):
    need = 2 * block_bytes + scratch_bytes
    limit = min(VMEM_CAP_BYTES, max(need + need // 4, 32 * 1024 * 1024))
    return pltpu.CompilerParams(
        dimension_semantics=("arbitrary",) * n_grid, vmem_limit_bytes=limit)


def _rmsnorm_kernel(x_ref, g_ref, o_ref):
    x = x_ref[...]
    ms = jnp.mean(x * x, axis=-1, keepdims=True)
    o_ref[...] = (x * lax.rsqrt(ms + EPS) * g_ref[...]).astype(o_ref.dtype)


def rmsnorm(x, g, rows=512):
    t, d = x.shape
    blocks = _nbytes((rows, d), F32) + _nbytes((rows, d), BF16)
    return pl.pallas_call(
        _rmsnorm_kernel,
        grid=(t // rows,),
        in_specs=[pl.BlockSpec((rows, d), lambda i: (i, 0)),
                  pl.BlockSpec((1, d), lambda i: (0, 0))],
        out_specs=pl.BlockSpec((rows, d), lambda i: (i, 0)),
        out_shape=jax.ShapeDtypeStruct((t, d), BF16),
        compiler_params=_params(1, blocks, _nbytes((rows, d), F32)),
        name="rmsnorm",
    )(x, g.reshape(1, d))


def _cast_block_rows(rows, n_steps):
    rb = 16
    while rows // rb > n_steps:
        rb *= 2
    assert rows % rb == 0
    return rb


def _side_cast_specs(casts, n_steps, step_index):
    in_specs, out_specs, out_shapes, nbytes = [], [], [], 0
    for src, layer in casts:
        _, rows, cols = src.shape
        rb = _cast_block_rows(rows, n_steps)
        last = rows // rb - 1

        def in_map(*g, layer=layer, last=last):
            return (layer, jnp.minimum(step_index(*g), last), 0)

        def out_map(*g, last=last):
            return (jnp.minimum(step_index(*g), last), 0)

        in_specs.append(pl.BlockSpec((1, rb, cols), in_map))
        out_specs.append(pl.BlockSpec((rb, cols), out_map))
        out_shapes.append(jax.ShapeDtypeStruct((rows, cols), BF16))
        nbytes += _nbytes((rb, cols), F32) + _nbytes((rb, cols), BF16)
    return in_specs, out_specs, out_shapes, nbytes


def _run_side_casts(src_refs, dst_refs):
    for src, dst in zip(src_refs, dst_refs):
        dst[...] = src[0].astype(BF16)


Q_PRESCALE = HEAD_DIM ** -0.5 * math.log2(math.e)
ROPE_ROWS = 512


def _head_epilogue(r, gain_ref, cos_t_ref, sin_t_ref, o_ref, *, kind):
    half = HEAD_DIM // 2
    gain = gain_ref[...]
    for f in range(r.shape[1] // HEAD_DIM):
        cols = slice(f * HEAD_DIM, (f + 1) * HEAD_DIM)
        for t in range(r.shape[0] // ROPE_ROWS):
            rows = slice(t * ROPE_ROWS, (t + 1) * ROPE_ROWS)
            xt = r[rows, cols].T
            yt = xt * lax.rsqrt(jnp.mean(xt * xt, axis=0, keepdims=True) + EPS) * gain
            swapped = jnp.concatenate([yt[half:], yt[:half]], axis=0)
            out_t = yt * cos_t_ref[t] + swapped * sin_t_ref[t]
            if kind == "q":
                o_ref[rows, cols] = (out_t * Q_PRESCALE).T.astype(o_ref.dtype)
            else:
                o_ref[0, f, t] = out_t.astype(o_ref.dtype)


def _row_scale(ss_ref):
    return lax.rsqrt(jnp.sum(ss_ref[...], axis=-1, keepdims=True) * (1.0 / D_MODEL) + EPS)


def _emit_next_norm(x, gain_ref, xg_ref, ss_ref, first):
    xg_ref[...] = (x * gain_ref[...]).astype(xg_ref.dtype)
    sq = x * x
    part = functools.reduce(
        jnp.add, [sq[:, t * LANES:(t + 1) * LANES] for t in range(x.shape[1] // LANES)])

    @pl.when(first)
    def _():
        ss_ref[...] = part

    @pl.when(jnp.logical_not(first))
    def _():
        ss_ref[...] += part


def _matmul_kernel(*refs, act, epilogue, has_gate, has_add, has_scale, has_norm, n_cast):
    a_ref, w_ref = refs[0], refs[1]
    pos = 2
    gate_ref = add_ref = scale_ref = gain_ref = None
    if has_gate:
        gate_ref = refs[pos]
        pos += 1
    if has_add:
        add_ref = refs[pos]
        pos += 1
    if has_scale:
        scale_ref = refs[pos]
        pos += 1
    if has_norm:
        gain_ref = refs[pos]
        pos += 1
    head_refs = refs[pos:pos + 3] if epilogue else ()
    pos += len(head_refs)
    src_refs = refs[pos:pos + n_cast]
    pos += n_cast
    o_ref = refs[pos]
    norm_refs = refs[pos + 1:pos + 3] if has_norm else ()
    dst_refs = refs[pos + 1 + len(norm_refs):]
    _run_side_casts(src_refs, dst_refs)
    r = jnp.dot(a_ref[...], w_ref[...], preferred_element_type=F32)
    if has_scale:
        r = r * _row_scale(scale_ref)
    if epilogue:
        _head_epilogue(r, *head_refs, o_ref, kind=epilogue)
        return
    if act == "relu2":
        r = jnp.square(jnp.maximum(r, 0.0))
    if has_gate:
        r = r * jax.nn.sigmoid(gate_ref[...].astype(F32))
    if has_add:
        r = r + add_ref[...].astype(F32)
    o_ref[...] = r.astype(o_ref.dtype)
    if has_norm:
        _emit_next_norm(r, gain_ref, *norm_refs, pl.program_id(1) == 0)


def matmul(a, w, *, out_dtype, n=None, w_col=0, act=None, gate=None, gate_col=0, add=None,
           row_ss=None, next_gain=None, epilogue=None, head=None, seq=None, casts=(),
           bm=1024, bn=1024, name="matmul"):
    m, kdim = a.shape
    n = w.shape[1] if n is None else n
    assert m % bm == 0 and n % bn == 0 and gate_col % bn == 0 and w_col % bn == 0
    nj = n // bn
    o_map = lambda i, j: (i, j)
    in_specs = [pl.BlockSpec((bm, kdim), lambda i, j: (i, 0)),
                pl.BlockSpec((kdim, bn), lambda i, j: (0, w_col // bn + j))]
    args = [a, w]
    blocks = (_nbytes((bm, kdim), a.dtype) + _nbytes((kdim, bn), w.dtype)
              + _nbytes((bm, bn), out_dtype))
    if gate is not None:
        in_specs.append(pl.BlockSpec((bm, bn), lambda i, j: (i, gate_col // bn + j)))
        args.append(gate)
        blocks += _nbytes((bm, bn), gate.dtype)
    if add is not None:
        in_specs.append(pl.BlockSpec((bm, bn), o_map))
        args.append(add)
        blocks += _nbytes((bm, bn), add.dtype)
    ss_spec = pl.BlockSpec((bm, LANES), lambda i, j: (i, 0))
    if row_ss is not None:
        in_specs.append(ss_spec)
        args.append(row_ss)
        blocks += _nbytes((bm, LANES), F32)
    n_out, n_shape = [], []
    if next_gain is not None:
        assert n == D_MODEL and out_dtype == F32
        in_specs.append(pl.BlockSpec((1, bn), lambda i, j: (0, j)))
        args.append(next_gain.reshape(1, n))
        n_out = [pl.BlockSpec((bm, bn), o_map), ss_spec]
        n_shape = [jax.ShapeDtypeStruct((m, n), BF16), jax.ShapeDtypeStruct((m, LANES), F32)]
        blocks += _nbytes((bm, bn), BF16) + _nbytes((bm, LANES), F32)
    out_spec = pl.BlockSpec((bm, bn), o_map)
    out_shape = jax.ShapeDtypeStruct((m, n), out_dtype)
    if epilogue:
        gain, cos_t, sin_t = head
        assert bm % ROPE_ROWS == 0
        tb = bm // ROPE_ROWS
        tab_spec = pl.BlockSpec((tb, HEAD_DIM, ROPE_ROWS), lambda i, j: (i, 0, 0))
        in_specs += [pl.BlockSpec((HEAD_DIM, 1), lambda i, j: (0, 0)), tab_spec, tab_spec]
        args += [gain.reshape(HEAD_DIM, 1), cos_t, sin_t]
        blocks += 2 * _nbytes((bm, HEAD_DIM), F32)
    if epilogue == "k":
        assert seq % bm == 0
        per_seq = seq // bm
        out_spec = pl.BlockSpec((1, bn // HEAD_DIM, tb, HEAD_DIM, ROPE_ROWS),
                                lambda i, j: (i // per_seq, j, i % per_seq, 0, 0))
        out_shape = jax.ShapeDtypeStruct(
            (m // seq, n // HEAD_DIM, seq // ROPE_ROWS, HEAD_DIM, ROPE_ROWS), out_dtype)
    c_in, c_out, c_shapes, c_bytes = _side_cast_specs(
        casts, (m // bm) * nj, lambda i, j: i * nj + j)
    kern = functools.partial(_matmul_kernel, act=act, epilogue=epilogue,
                             has_gate=gate is not None, has_add=add is not None,
                             has_scale=row_ss is not None, has_norm=next_gain is not None,
                             n_cast=len(casts))
    out = pl.pallas_call(
        kern,
        grid=(m // bm, nj),
        in_specs=in_specs + c_in,
        out_specs=[out_spec] + n_out + c_out,
        out_shape=[out_shape] + n_shape + c_shapes,
        compiler_params=_params(2, blocks + c_bytes, 2 * _nbytes((bm, bn), F32)),
        name=name,
    )(*args, *[src for src, _ in casts])
    return out if len(out) > 1 else out[0]


def _matmul_ktiled_kernel(*refs, has_norm, n_cast):
    a_ref, w_ref, add_ref = refs[:3]
    pos = 3
    gain_ref = None
    if has_norm:
        gain_ref = refs[pos]
        pos += 1
    src_refs = refs[pos:pos + n_cast]
    pos += n_cast
    o_ref = refs[pos]
    norm_refs = refs[pos + 1:pos + 3] if has_norm else ()
    dst_refs = refs[pos + 1 + len(norm_refs):]
    _run_side_casts(src_refs, dst_refs)

    @pl.when(pl.program_id(2) == 0)
    def _():
        o_ref[...] = add_ref[...]

    o_ref[...] += jnp.dot(a_ref[...], w_ref[...], preferred_element_type=F32)

    if has_norm:
        @pl.when(pl.program_id(2) == pl.num_programs(2) - 1)
        def _():
            _emit_next_norm(o_ref[...], gain_ref, *norm_refs, pl.program_id(1) == 0)


def matmul_ktiled(a, w, add, *, next_gain=None, casts=(), bm=1024, bn=512, bk=4096,
                  name="matmul_ktiled"):
    m, kdim = a.shape
    n = w.shape[1]
    assert m % bm == 0 and n % bn == 0 and kdim % bk == 0 and add.dtype == F32
    nj, nk = n // bn, kdim // bk
    o_map = lambda i, j, k: (i, j)
    blocks = (_nbytes((bm, bk), a.dtype) + _nbytes((bk, bn), w.dtype)
              + 2 * _nbytes((bm, bn), F32))
    in_specs = [pl.BlockSpec((bm, bk), lambda i, j, k: (i, k)),
                pl.BlockSpec((bk, bn), lambda i, j, k: (k, j)),
                pl.BlockSpec((bm, bn), o_map)]
    args = [a, w, add]
    n_out, n_shape = [], []
    if next_gain is not None:
        assert n == D_MODEL
        in_specs.append(pl.BlockSpec((1, bn), lambda i, j, k: (0, j)))
        args.append(next_gain.reshape(1, n))
        n_out = [pl.BlockSpec((bm, bn), o_map), pl.BlockSpec((bm, LANES), lambda i, j, k: (i, 0))]
        n_shape = [jax.ShapeDtypeStruct((m, n), BF16), jax.ShapeDtypeStruct((m, LANES), F32)]
        blocks += _nbytes((bm, bn), BF16) + _nbytes((bm, LANES), F32)
    c_in, c_out, c_shapes, c_bytes = _side_cast_specs(
        casts, (m // bm) * nj * nk, lambda i, j, k: (i * nj + j) * nk + k)
    out = pl.pallas_call(
        functools.partial(_matmul_ktiled_kernel, has_norm=next_gain is not None,
                          n_cast=len(casts)),
        grid=(m // bm, nj, nk),
        in_specs=in_specs + c_in,
        out_specs=[pl.BlockSpec((bm, bn), o_map)] + n_out + c_out,
        out_shape=[jax.ShapeDtypeStruct((m, n), F32)] + n_shape + c_shapes,
        compiler_params=_params(3, blocks + c_bytes, 2 * _nbytes((bm, bn), F32)),
        name=name,
    )(*args, *[src for src, _ in casts])
    return out if len(out) > 1 else out[0]


def _cast_kernel(src_ref, dst_ref):
    dst_ref[...] = src_ref[0].astype(BF16)


def cast_layer(w, layer, rows=64):
    _, kdim, n = w.shape
    blocks = _nbytes((rows, n), F32) + _nbytes((rows, n), BF16)
    return pl.pallas_call(
        _cast_kernel,
        grid=(kdim // rows,),
        in_specs=[pl.BlockSpec((1, rows, n), lambda i: (layer, i, 0))],
        out_specs=pl.BlockSpec((rows, n), lambda i: (i, 0)),
        out_shape=jax.ShapeDtypeStruct((kdim, n), BF16),
        compiler_params=_params(1, blocks),
        name="cast_weights",
    )(w)


def _rope_table_kernel(pos_ref, freq_ref, cos_t_ref, sin_t_ref):
    ang_t = freq_ref[...] * pos_ref[0].astype(F32)
    sub = lax.broadcasted_iota(jnp.int32, ang_t.shape, 0)
    cos_t_ref[0] = jnp.cos(ang_t)
    sin_t_ref[0] = jnp.sin(ang_t) * jnp.where(sub < HEAD_DIM // 2, -1.0, 1.0)


def rope_tables(positions):
    nb = positions.size // ROPE_ROWS
    inv_freq = ROPE_THETA ** (-jnp.arange(0, HEAD_DIM, 2, dtype=F32) / HEAD_DIM)
    freq = jnp.concatenate([inv_freq, inv_freq]).reshape(HEAD_DIM, 1)
    tab = jax.ShapeDtypeStruct((nb, HEAD_DIM, ROPE_ROWS), F32)
    return pl.pallas_call(
        _rope_table_kernel,
        grid=(nb,),
        in_specs=[pl.BlockSpec((1, 1, ROPE_ROWS), lambda i: (i, 0, 0)),
                  pl.BlockSpec((HEAD_DIM, 1), lambda i: (0, 0))],
        out_specs=[pl.BlockSpec((1, HEAD_DIM, ROPE_ROWS), lambda i: (i, 0, 0))] * 2,
        out_shape=[tab, tab],
        compiler_params=_params(1, 3 * _nbytes((ROPE_ROWS, HEAD_DIM), F32)),
        name="rope_tables",
    )(positions.reshape(nb, 1, ROPE_ROWS), freq)


def _gmlp_kernel(z_ref, w_ref, b_ref, g_ref, beta_ref, o_ref, vn_ref, *, rows):
    v = jax.nn.gelu(z_ref[:, GM_WIDTH:].astype(F32))
    mu = jnp.mean(v, axis=-1, keepdims=True)
    vc = v - mu
    var = jnp.mean(vc * vc, axis=-1, keepdims=True)
    vn = vc * lax.rsqrt(var + EPS) * g_ref[...] + beta_ref[...]
    vn_ref[...] = vn.astype(BF16)
    t_idx = lax.broadcasted_iota(jnp.int32, (CHUNK, CHUNK), 0)
    s_idx = lax.broadcasted_iota(jnp.int32, (CHUNK, CHUNK), 1)
    causal = s_idx <= t_idx
    for g in range(GM_GROUPS):
        w = jnp.where(causal, w_ref[g], 0.0).astype(BF16)
        bias = b_ref[:, g:g + 1]
        cols = slice(g * GM_GROUP_DIM, (g + 1) * GM_GROUP_DIM)
        for c in range(rows // CHUNK):
            rs = slice(c * CHUNK, (c + 1) * CHUNK)
            mixed = jnp.dot(w, vn_ref[rs, cols], preferred_element_type=F32) + bias
            u = jax.nn.gelu(z_ref[rs, cols].astype(F32))
            o_ref[rs, cols] = (u * mixed).astype(o_ref.dtype)


def gmlp_gating(z, w_s, b_s, ln_g, ln_b, rows=256):
    t = z.shape[0]
    blocks = _nbytes((rows, 2 * GM_WIDTH), BF16) + _nbytes((rows, GM_WIDTH), BF16)
    small = _nbytes(w_s.shape, F32) + _nbytes((CHUNK, 128), F32) + 2 * _nbytes((8, GM_WIDTH), F32)
    return pl.pallas_call(
        functools.partial(_gmlp_kernel, rows=rows),
        grid=(t // rows,),
        in_specs=[pl.BlockSpec((rows, 2 * GM_WIDTH), lambda i: (i, COL_U)),
                  pl.BlockSpec(w_s.shape, lambda i: (0, 0, 0)),
                  pl.BlockSpec((CHUNK, GM_GROUPS), lambda i: (0, 0)),
                  pl.BlockSpec((1, GM_WIDTH), lambda i: (0, 0)),
                  pl.BlockSpec((1, GM_WIDTH), lambda i: (0, 0))],
        out_specs=pl.BlockSpec((rows, GM_WIDTH), lambda i: (i, 0)),
        out_shape=jax.ShapeDtypeStruct((t, GM_WIDTH), BF16),
        scratch_shapes=[pltpu.VMEM((rows, GM_WIDTH), BF16)],
        compiler_params=_params(1, blocks + small, 4 * _nbytes((rows, GM_WIDTH), F32)),
        name="gmlp_gating",
    )(z, w_s, b_s.T, ln_g.reshape(1, GM_WIDTH), ln_b.reshape(1, GM_WIDTH))


ATTN_BQ = 2048
ATTN_BK = ROPE_ROWS
ATTN_RQ = 128
ATTN_UNROLL = 4


def _attn_kernel(lam_ref, q_ref, kt_ref, v_ref, g_ref, o_ref, m_ref, l_ref, acc_ref,
                 *, lambda_init):
    bq, bk, rq = ATTN_BQ, ATTN_BK, ATTN_RQ
    i = pl.program_id(2)
    m_ref[...] = jnp.full(m_ref.shape, NEG_INF, F32)
    l_ref[...] = jnp.zeros(l_ref.shape, F32)
    acc_ref[...] = jnp.zeros(acc_ref.shape, F32)

    def chain(j, c, r, masked):
        kv_rows = pl.ds(pl.multiple_of(j * bk, bk), bk)
        rs = slice(r * rq, (r + 1) * rq)
        q = q_ref[rs, c * HEAD_DIM:(c + 1) * HEAD_DIM]
        s = jnp.dot(q, kt_ref[0, c, j], preferred_element_type=F32)
        if masked:
            qi = lax.broadcasted_iota(jnp.int32, s.shape, 0) + (r * rq) % bk
            ki = lax.broadcasted_iota(jnp.int32, s.shape, 1)
            s = jnp.where(ki <= qi, s, NEG_INF)
        m_prev = m_ref[c, rs, :]
        m_new = jnp.maximum(m_prev, jnp.max(s, axis=-1, keepdims=True))
        alpha = jnp.exp2(m_prev - m_new)
        parts = [jnp.exp2(s[:, t * LANES:(t + 1) * LANES] - m_new) for t in range(bk // LANES)]
        l_ref[c, rs, :] = alpha * l_ref[c, rs, :] + functools.reduce(jnp.add, parts)
        p = jnp.concatenate(parts, axis=-1).astype(BF16)
        pv = jnp.dot(p, v_ref[kv_rows, :], preferred_element_type=F32)
        alpha_v = jnp.concatenate([alpha] * (V_DIM // LANES), axis=-1)
        acc_ref[c, rs, :] = alpha_v * acc_ref[c, rs, :] + pv
        m_ref[c, rs, :] = m_new

    n_diag = bq // bk
    n_full = i * n_diag

    def body(jj, carry):
        for u in range(ATTN_UNROLL):
            for r in range(bq // rq):
                for c in range(2):
                    chain(jj * ATTN_UNROLL + u, c, r, False)
        return carry

    lax.fori_loop(0, n_full // ATTN_UNROLL, body, 0)

    for d in range(n_diag):
        for r in range(bq // rq):
            row_lo, row_hi = r * rq, (r + 1) * rq - 1
            col_lo, col_hi = d * bk, (d + 1) * bk - 1
            if row_hi < col_lo:
                continue
            for c in range(2):
                chain(n_full + d, c, r, col_hi > row_lo)

    lp = lam_ref[...]
    lam = (jnp.exp(jnp.sum(lp[0:1] * lp[1:2], axis=-1, keepdims=True))
           - jnp.exp(jnp.sum(lp[2:3] * lp[3:4], axis=-1, keepdims=True))
           + lambda_init)
    l0 = jnp.sum(l_ref[0], axis=-1, keepdims=True)
    l1 = jnp.sum(l_ref[1], axis=-1, keepdims=True)
    o = acc_ref[0] / l0 - lam * (acc_ref[1] / l1)
    ms = jnp.mean(o * o, axis=-1, keepdims=True)
    o = o * lax.rsqrt(ms + EPS) * g_ref[...]
    o_ref[...] = (o * (1.0 - lambda_init)).astype(o_ref.dtype)


def diff_attention(q, kt, va, lam_params, subln_g, *, batch, seq, lambda_init):
    bq, bk = ATTN_BQ, ATTN_BK
    assert bk % ATTN_RQ == 0 and bq % bk == 0
    assert (bq // bk) % ATTN_UNROLL == 0 and seq % bq == 0
    t = batch * seq
    nq = seq // bq
    nk = seq // bk
    blocks = (2 * _nbytes((bq, V_DIM), BF16) + 2 * _nbytes((seq, V_DIM), BF16))
    scratch = 2 * _nbytes((bq, V_DIM), F32) + 4 * _nbytes((bq, LANES), F32)
    kern = functools.partial(_attn_kernel, lambda_init=lambda_init)
    return pl.pallas_call(
        kern,
        grid=(batch, HEADS, nq),
        in_specs=[pl.BlockSpec((4, HEAD_DIM), lambda b, h, i: (0, 0)),
                  pl.BlockSpec((bq, V_DIM), lambda b, h, i: (b * nq + i, h)),
                  pl.BlockSpec((1, 2, nk, HEAD_DIM, bk), lambda b, h, i: (b, h, 0, 0, 0)),
                  pl.BlockSpec((seq, V_DIM), lambda b, h, i: (b, h)),
                  pl.BlockSpec((1, V_DIM), lambda b, h, i: (0, 0))],
        out_specs=pl.BlockSpec((bq, V_DIM), lambda b, h, i: (b * nq + i, h)),
        out_shape=jax.ShapeDtypeStruct((t, DA_WIDTH), BF16),
        scratch_shapes=[pltpu.VMEM((2, bq, LANES), F32),
                        pltpu.VMEM((2, bq, LANES), F32),
                        pltpu.VMEM((2, bq, V_DIM), F32)],
        compiler_params=_params(3, blocks, 2 * scratch),
        name="diff_attention",
    )(lam_params, q, kt, va, subln_g.reshape(1, V_DIM))


def kernel(x, positions, mix_norm_g, w_in, gm_w_s, gm_b_s, gm_ln_g, gm_ln_b, q_norm_g, k_norm_g,
           lambda_q1, lambda_k1, lambda_q2, lambda_k2, subln_g, w_proj_a, w_proj_b, w_out,
           mlp_norm_g, w_up, w_down):
    batch, seq, d = x.shape
    depth = w_in.shape[0]
    t = batch * seq
    xf = x.reshape(t, d)
    cos_t, sin_t = rope_tables(positions)
    w_in_bf = cast_layer(w_in, 0)
    h, ss = rmsnorm(xf, mix_norm_g[0]), None
    for l in range(depth):
        lambda_init = 0.8 - 0.6 * math.exp(-0.3 * l)
        last = l + 1 == depth
        z_uv = matmul(h, w_in_bf, out_dtype=BF16, n=2 * GM_WIDTH, w_col=COL_U, row_ss=ss,
                      name="in_proj_uv")
        q = matmul(h, w_in_bf, out_dtype=BF16, n=QK_WIDTH, w_col=COL_Q, row_ss=ss, epilogue="q",
                   head=(q_norm_g[l], cos_t, sin_t), name="in_proj_q")
        kt = matmul(h, w_in_bf, out_dtype=BF16, n=QK_WIDTH, w_col=COL_K, row_ss=ss, epilogue="k",
                    head=(k_norm_g[l], cos_t, sin_t), seq=seq, name="in_proj_k")
        z_tail, w_up_bf, w_pa_bf = matmul(h, w_in_bf, out_dtype=BF16, n=IN_COLS - COL_VA,
                                          w_col=COL_VA, row_ss=ss,
                                          casts=[(w_up, l), (w_proj_a, l)], name="in_proj_tail")
        ya = gmlp_gating(z_uv, gm_w_s[l], gm_b_s[l], gm_ln_g[l], gm_ln_b[l])
        lam_params = jnp.stack([lambda_q1[l], lambda_k1[l], lambda_q2[l], lambda_k2[l]])
        o = diff_attention(q, kt, z_tail, lam_params, subln_g[l], batch=batch, seq=seq,
                           lambda_init=lambda_init)
        ga, w_pb_bf = matmul(ya, w_pa_bf, out_dtype=F32, gate=z_tail, gate_col=COL_GA - COL_VA,
                             casts=[(w_proj_b, l)], name="proj_a_gate")
        merged, w_out_bf = matmul(o, w_pb_bf, out_dtype=BF16, gate=z_tail,
                                  gate_col=COL_GB - COL_VA, add=ga,
                                  casts=[(w_out, l)], bn=512, name="proj_b_gate_merge")
        xf, h, ss = matmul(merged, w_out_bf, out_dtype=F32, add=xf, next_gain=mlp_norm_g[l],
                           bn=512, name="out_proj")
        up_casts = [(w_down, l)] + ([] if last else [(w_in, l + 1)])
        hid, w_down_bf, *w_next = matmul(h, w_up_bf, out_dtype=BF16, act="relu2", row_ss=ss,
                                         casts=up_casts, name="mlp_up")
        if last:
            xf = matmul_ktiled(hid, w_down_bf, xf, name="mlp_down")
        else:
            w_in_bf = w_next[0]
            xf, h, ss = matmul_ktiled(hid, w_down_bf, xf, next_gain=mix_norm_g[l + 1],
                                      name="mlp_down")
    return xf.reshape(batch, seq, d)
```
